```python
import math
import jax, jax.numpy as jnp
from jax import lax
import numpy as np

D_MODEL = 1024
BATCH = 1
SEQ = 16384
DEPTH = 2
DEC_BATCH = 16
DEC_SEQ = 16
PAST_LEN = 2048

CHUNK = 64
N_META = 16
N_A_LAYERS = DEPTH // 2
N_B_LAYERS = DEPTH - N_A_LAYERS
SSM_GROUP = 16
SSM_GROUPS = D_MODEL // SSM_GROUP
SSM_STATE = 64
DT_MIN = 0.001
DT_MAX = 0.1
N_HEADS = 16
HEAD_DIM = D_MODEL // N_HEADS
ATTN_SCALE = HEAD_DIM ** -0.5
Q_BLOCK = 128
PEER_HEADS = 8
PEER_NKEYS = 128
PEER_EXPERTS = PEER_NKEYS * PEER_NKEYS
PEER_TOPK = 16
PEER_DKEY = 256
PEER_HALF = PEER_DKEY // 2
PEER_BLOCK = 256
LN_EPS = 1e-5
DN_ALPHA = (2.0 * DEPTH) ** 0.25
DN_BETA = (8.0 * DEPTH) ** -0.25

kernel_name = "s5_fox_peer_yoco_stream_step"


def layer_norm(x, g, b):
    xf = x.astype(jnp.float32)
    mu = jnp.mean(xf, axis=-1, keepdims=True)
    var = jnp.mean(jnp.square(xf - mu), axis=-1, keepdims=True)
    return ((xf - mu) * lax.rsqrt(var + LN_EPS) * g.astype(jnp.float32) + b.astype(jnp.float32)).astype(x.dtype)


def ssm_discretise(lam_re, lam_im, log_dt, b_re, b_im):
    lam_re = lam_re.astype(jnp.float32)
    lam_im = lam_im.astype(jnp.float32)
    dt = jnp.exp(log_dt.astype(jnp.float32))[:, None]
    mag = jnp.exp(lam_re * dt)
    ab_re = mag * jnp.cos(lam_im * dt)
    ab_im = mag * jnp.sin(lam_im * dt)
    n_re = ab_re - 1.0
    den = lam_re * lam_re + lam_im * lam_im
    f_re = (n_re * lam_re + ab_im * lam_im) / den
    f_im = (ab_im * lam_re - n_re * lam_im) / den
    b_re = b_re.astype(jnp.float32)
    b_im = b_im.astype(jnp.float32)
    bb_re = f_re[..., None] * b_re - f_im[..., None] * b_im
    bb_im = f_re[..., None] * b_im + f_im[..., None] * b_re
    return ab_re, ab_im, bb_re, bb_im


def _ssm_combine(e1, e2):
    a1r, a1i, b1r, b1i = e1
    a2r, a2i, b2r, b2i = e2
    return (a2r * a1r - a2i * a1i, a2r * a1i + a2i * a1r,
            a2r * b1r - a2i * b1i + b2r, a2r * b1i + a2i * b1r + b2i)


def ssm_block(u, h_re, h_im, ab_re, ab_im, bb_re, bb_im, c_re, c_im, d_skip):
    bu_re = jnp.einsum("btgc,gpc->btgp", u, bb_re)
    bu_im = jnp.einsum("btgc,gpc->btgp", u, bb_im)
    bu_re = bu_re.at[:, 0].add(ab_re * h_re - ab_im * h_im)
    bu_im = bu_im.at[:, 0].add(ab_re * h_im + ab_im * h_re)
    a_re = jnp.broadcast_to(ab_re, bu_re.shape)
    a_im = jnp.broadcast_to(ab_im, bu_im.shape)
    _, _, s_re, s_im = lax.associative_scan(_ssm_combine, (a_re, a_im, bu_re, bu_im), axis=1)
    y = (jnp.einsum("btgp,gcp->btgc", s_re, c_re) - jnp.einsum("btgp,gcp->btgc", s_im, c_im)
         + d_skip * u)
    return y, s_re[:, -1], s_im[:, -1]


def s5_mixer(x, h_re, h_im, lam_re, lam_im, log_dt, b_re, b_im, c_re, c_im, d_skip, w_glu, block):
    Bsz, L, _ = x.shape
    pad = (-L) % block
    nc = (L + pad) // block
    ab_re, ab_im, bb_re, bb_im = ssm_discretise(lam_re, lam_im, log_dt, b_re, b_im)
    c_re = c_re.astype(jnp.float32)
    c_im = c_im.astype(jnp.float32)
    d_skip = d_skip.astype(jnp.float32)
    u = jnp.pad(x.astype(jnp.float32), ((0, 0), (pad, 0), (0, 0)))
    u = u.reshape(Bsz, nc, block, SSM_GROUPS, SSM_GROUP).swapaxes(0, 1)

    def step(carry, u_c):
        y_c, hr, hi = ssm_block(u_c, carry[0], carry[1], ab_re, ab_im, bb_re, bb_im, c_re, c_im, d_skip)
        return (hr, hi), y_c

    (hr, hi), ys = lax.scan(step, (h_re.astype(jnp.float32), h_im.astype(jnp.float32)), u)
    y = ys.swapaxes(0, 1).reshape(Bsz, nc * block, D_MODEL)[:, pad:]
    y = jax.nn.gelu(y).astype(x.dtype)
    z = y @ w_glu
    out = z[..., :D_MODEL] * jax.nn.sigmoid(z[..., D_MODEL:])
    return out, hr, hi


def shared_kv(x, w_kvf, b_f):
    Bsz, T, _ = x.shape
    z = x @ w_kvf
    k = z[..., :D_MODEL].reshape(Bsz, T, N_HEADS, HEAD_DIM)
    v = z[..., D_MODEL:2 * D_MODEL].reshape(Bsz, T, N_HEADS, HEAD_DIM)
    logf = jax.nn.log_sigmoid((z[..., 2 * D_MODEL:] + b_f).astype(jnp.float32))
    return k, v, logf


def fox_attend(q, k, v, cq, ck, qpos, kpos):
    Bsz, Tq = q.shape[0], q.shape[1]
    blk = min(Q_BLOCK, Tq)
    pad = (-Tq) % blk
    nb = (Tq + pad) // blk

    def blocks(a):
        a = jnp.pad(a, [(0, 0), (0, pad)] + [(0, 0)] * (a.ndim - 2), mode="edge")
        return a.reshape((Bsz, nb, blk) + a.shape[2:]).swapaxes(0, 1)

    q_b = blocks(q)
    cq_b = blocks(cq)
    qpos_b = jnp.pad(qpos, (0, pad), mode="edge").reshape(nb, blk)
    ck_t = ck.swapaxes(1, 2)

    def one_block(args):
        q_i, cq_i, qpos_i = args
        s = jnp.einsum("bqhd,bkhd->bhqk", q_i, k, preferred_element_type=jnp.float32) * ATTN_SCALE
        s = s + cq_i.swapaxes(1, 2)[..., None] - ck_t[:, :, None, :]
        s = jnp.where(kpos[None, :] <= qpos_i[:, None], s, -jnp.inf)
        p = jax.nn.softmax(s, axis=-1)
        return jnp.einsum("bhqk,bkhd->bqhd", p.astype(v.dtype), v)

    o = lax.map(one_block, (q_b, cq_b, qpos_b))
    return o.swapaxes(0, 1).reshape(Bsz, nb * blk, N_HEADS, HEAD_DIM)[:, :Tq]


def fox_mixer(x, k, v, ck, w_q, w_o):
    Bsz, Tq, _ = x.shape
    Tk = k.shape[1]
    q = (x @ w_q).reshape(Bsz, Tq, N_HEADS, HEAD_DIM)
    kpos = jnp.arange(Tk, dtype=jnp.int32)
    o = fox_attend(q, k, v, ck[:, Tk - Tq:], ck, kpos[Tk - Tq:], kpos)
    return o.reshape(Bsz, Tq, D_MODEL) @ w_o


def peer_ffn(x, w_pq, sub_k1, sub_k2, u_tab, v_tab):
    Bsz, T, _ = x.shape
    n_tok = Bsz * T
    blk = min(PEER_BLOCK, n_tok)
    pad = (-n_tok) % blk
    xb = jnp.pad(x.reshape(n_tok, D_MODEL), ((0, pad), (0, 0))).reshape(-1, blk, D_MODEL)
    k1 = sub_k1.astype(jnp.float32)
    k2 = sub_k2.astype(jnp.float32)

    def one_block(xi):
        q = (xi @ w_pq).astype(jnp.float32).reshape(blk, PEER_HEADS, 2, PEER_HALF)
        s1 = jnp.einsum("nhc,kc->nhk", q[:, :, 0], k1)
        s2 = jnp.einsum("nhc,kc->nhk", q[:, :, 1], k2)
        v1, i1 = lax.top_k(s1, PEER_TOPK)
        v2, i2 = lax.top_k(s2, PEER_TOPK)
        cand = (v1[..., :, None] + v2[..., None, :]).reshape(blk, PEER_HEADS, PEER_TOPK * PEER_TOPK)
        cidx = (i1[..., :, None] * PEER_NKEYS + i2[..., None, :]).reshape(blk, PEER_HEADS, PEER_TOPK * PEER_TOPK)
        sc, j = lax.top_k(cand, PEER_TOPK)
        idx = jnp.take_along_axis(cidx, j, axis=-1)
        g = jax.nn.softmax(sc, axis=-1)
        u = jnp.take(u_tab, idx, axis=0)
        act = jax.nn.gelu(jnp.einsum("nhkd,nd->nhk", u, xi, preferred_element_type=jnp.float32))
        w = (g * act).astype(xi.dtype)
        vv = jnp.take(v_tab, idx, axis=0)
        return jnp.einsum("nhk,nhkd->nd", w, vv)

    out = lax.map(one_block, xb).reshape(-1, D_MODEL)[:n_tok]
    return out.reshape(Bsz, T, D_MODEL)


def trunk(x, h_re, h_im, k_past, v_past, lf_past, ssm_blk,
          ssm_lam_re, ssm_lam_im, ssm_log_dt, ssm_b_re, ssm_b_im, ssm_c_re, ssm_c_im, ssm_d, w_glu,
          w_kvf, b_f, w_q, w_o, peer_w_q, peer_k1, peer_k2, peer_u, peer_v,
          ln1_g, ln1_b, ln2_g, ln2_b):
    new_re, new_im = [], []
    k_new = v_new = lf_new = k_all = v_all = ck = None
    for layer in range(DEPTH):
        if layer < N_A_LAYERS:
            mix, hr, hi = s5_mixer(x, h_re[:, layer], h_im[:, layer], ssm_lam_re[layer], ssm_lam_im[layer],
                                   ssm_log_dt[layer], ssm_b_re[layer], ssm_b_im[layer], ssm_c_re[layer],
                                   ssm_c_im[layer], ssm_d[layer], w_glu[layer], ssm_blk)
            new_re.append(hr)
            new_im.append(hi)
        else:
            if layer == N_A_LAYERS:
                k_new, v_new, lf_new = shared_kv(x, w_kvf, b_f)
                if k_past is None:
                    k_all, v_all, lf_all = k_new, v_new, lf_new
                else:
                    k_all = jnp.concatenate([k_past.astype(k_new.dtype), k_new], axis=1)
                    v_all = jnp.concatenate([v_past.astype(v_new.dtype), v_new], axis=1)
                    lf_all = jnp.concatenate([lf_past.astype(jnp.float32), lf_new], axis=1)
                ck = jnp.cumsum(lf_all, axis=1)
            jb = layer - N_A_LAYERS
            mix = fox_mixer(x, k_all, v_all, ck, w_q[jb], w_o[jb])
        x = layer_norm(DN_ALPHA * x + mix, ln1_g[layer], ln1_b[layer])
        ffn = peer_ffn(x, peer_w_q[layer], peer_k1[layer], peer_k2[layer], peer_u[layer], peer_v[layer])
        x = layer_norm(DN_ALPHA * x + ffn, ln2_g[layer], ln2_b[layer])
    return x, jnp.stack(new_re, axis=1), jnp.stack(new_im, axis=1), k_new, v_new, lf_new


def setup_inputs(seed: int = 0) -> dict:
    key = jax.random.key(seed)
    ks = jax.random.split(key, 32)
    f32 = jnp.float32

    def nrm(k, shape, scale):
        return jax.random.normal(k, shape, f32) * scale

    n_idx = jnp.arange(SSM_STATE, dtype=f32)
    lam_re = -0.5 + nrm(ks[8], (N_A_LAYERS, SSM_GROUPS, SSM_STATE), 0.01)
    lam_im = math.pi * n_idx + nrm(ks[9], (N_A_LAYERS, SSM_GROUPS, SSM_STATE), 0.01)
    return {
        "x_prompt": nrm(ks[0], (BATCH, SEQ, D_MODEL), 1.0),
        "x_sample": nrm(ks[1], (DEC_BATCH, DEC_SEQ, D_MODEL), 1.0),
        "state_ssm_re": nrm(ks[2], (DEC_BATCH, N_A_LAYERS, SSM_GROUPS, SSM_STATE), 0.3),
        "state_ssm_im": nrm(ks[3], (DEC_BATCH, N_A_LAYERS, SSM_GROUPS, SSM_STATE), 0.3),
        "cache_k": nrm(ks[4], (DEC_BATCH, PAST_LEN, N_HEADS, HEAD_DIM), 1.0),
        "cache_v": nrm(ks[5], (DEC_BATCH, PAST_LEN, N_HEADS, HEAD_DIM), 1.0),
        "cache_logf": jax.nn.log_sigmoid(2.0 + nrm(ks[6], (DEC_BATCH, PAST_LEN, N_HEADS), 1.0)),
        "meta_tokens": nrm(ks[7], (N_META, D_MODEL), 1.0),
        "ssm_lam_re": lam_re,
        "ssm_lam_im": lam_im,
        "ssm_log_dt": jax.random.uniform(ks[10], (N_A_LAYERS, SSM_GROUPS), f32,
                                         minval=math.log(DT_MIN), maxval=math.log(DT_MAX)),
        "ssm_b_re": nrm(ks[11], (N_A_LAYERS, SSM_GROUPS, SSM_STATE, SSM_GROUP), (2 * SSM_GROUP) ** -0.5),
        "ssm_b_im": nrm(ks[12], (N_A_LAYERS, SSM_GROUPS, SSM_STATE, SSM_GROUP), (2 * SSM_GROUP) ** -0.5),
        "ssm_c_re": nrm(ks[13], (N_A_LAYERS, SSM_GROUPS, SSM_GROUP, SSM_STATE), SSM_STATE ** -0.5),
        "ssm_c_im": nrm(ks[14], (N_A_LAYERS, SSM_GROUPS, SSM_GROUP, SSM_STATE), SSM_STATE ** -0.5),
        "ssm_d": nrm(ks[15], (N_A_LAYERS, SSM_GROUPS, SSM_GROUP), 1.0),
        "w_glu": nrm(ks[16], (N_A_LAYERS, D_MODEL, 2 * D_MODEL), DN_BETA * D_MODEL ** -0.5),
        "w_kvf": nrm(ks[17], (D_MODEL, 2 * D_MODEL + N_HEADS), D_MODEL ** -0.5),
        "b_f": 2.0 + nrm(ks[18], (N_HEADS,), 0.1),
        "w_q": nrm(ks[19], (N_B_LAYERS, D_MODEL, D_MODEL), D_MODEL ** -0.5),
        "w_o": nrm(ks[20], (N_B_LAYERS, D_MODEL, D_MODEL), DN_BETA * D_MODEL ** -0.5),
        "peer_w_q": nrm(ks[21], (DEPTH, D_MODEL, PEER_HEADS * PEER_DKEY), D_MODEL ** -0.5),
        "peer_k1": nrm(ks[22], (DEPTH, PEER_NKEYS, PEER_HALF), PEER_HALF ** -0.5),
        "peer_k2": nrm(ks[23], (DEPTH, PEER_NKEYS, PEER_HALF), PEER_HALF ** -0.5),
        "peer_u": nrm(ks[24], (DEPTH, PEER_EXPERTS, D_MODEL), D_MODEL ** -0.5),
        "peer_v": nrm(ks[25], (DEPTH, PEER_EXPERTS, D_MODEL), DN_BETA * PEER_HEADS ** -0.5),
        "ln1_g": 1.0 + nrm(ks[26], (DEPTH, D_MODEL), 0.02),
        "ln1_b": nrm(ks[27], (DEPTH, D_MODEL), 0.02),
        "ln2_g": 1.0 + nrm(ks[28], (DEPTH, D_MODEL), 0.02),
        "ln2_b": nrm(ks[29], (DEPTH, D_MODEL), 0.02),
    }


def reference(x_prompt, x_sample, state_ssm_re, state_ssm_im, cache_k, cache_v, cache_logf, meta_tokens,
              ssm_lam_re, ssm_lam_im, ssm_log_dt, ssm_b_re, ssm_b_im, ssm_c_re, ssm_c_im, ssm_d, w_glu,
              w_kvf, b_f, w_q, w_o, peer_w_q, peer_k1, peer_k2, peer_u, peer_v,
              ln1_g, ln1_b, ln2_g, ln2_b):
    weights = (ssm_lam_re, ssm_lam_im, ssm_log_dt, ssm_b_re, ssm_b_im, ssm_c_re, ssm_c_im, ssm_d, w_glu,
               w_kvf, b_f, w_q, w_o, peer_w_q, peer_k1, peer_k2, peer_u, peer_v,
               ln1_g, ln1_b, ln2_g, ln2_b)
    bp = x_prompt.shape[0]
    meta = jnp.broadcast_to(meta_tokens.astype(x_prompt.dtype)[None], (bp, N_META, D_MODEL))
    xp = jnp.concatenate([meta, x_prompt], axis=1)
    h0 = jnp.zeros((bp, N_A_LAYERS, SSM_GROUPS, SSM_STATE), jnp.float32)
    yp, ssm_re_p, ssm_im_p, k_p, v_p, logf_p = trunk(xp, h0, h0, None, None, None, CHUNK, *weights)
    y_prompt = yp[:, N_META:]
    y_sample, ssm_re_s, ssm_im_s, k_s, v_s, logf_s = trunk(
        x_sample, state_ssm_re, state_ssm_im, cache_k, cache_v, cache_logf, x_sample.shape[1], *weights)
    return (y_prompt, y_sample, ssm_re_p, ssm_im_p, k_p, v_p, logf_p, ssm_re_s, ssm_im_s, k_s, v_s, logf_s)
```

```python
import functools
import math

import jax
import jax.numpy as jnp
from jax import lax
from jax.experimental import pallas as pl
from jax.experimental.pallas import tpu as pltpu

D_MODEL = 1024
N_META = 16
CHUNK = 64
SSM_GROUP = 16
SSM_GROUPS = D_MODEL // SSM_GROUP
SSM_STATE = 64
N_HEADS = 16
HEAD_DIM = D_MODEL // N_HEADS
ATTN_SCALE = HEAD_DIM ** -0.5
PEER_HEADS = 8
PEER_NKEYS = 128
PEER_TOPK = 16
PEER_DKEY = 256
PEER_HALF = PEER_DKEY // 2
LN_EPS = 1e-5
DEPTH = 2
DN_ALPHA = (2.0 * DEPTH) ** 0.25

VMEM_LIMIT_BYTES = 56 * 1024 * 1024
ROW_BLOCK = 512
PEER_ROWS = 256
PEER_I1_PER_STEP = 16
ATT_BLOCK = 512
NEG = -1e30

F32 = jnp.float32
BF16 = jnp.bfloat16
HI = lax.Precision.HIGHEST
NT = (((1,), (1,)), ((), ()))
TN = (((0,), (0,)), ((), ()))


def _params(*sem):
    return pltpu.CompilerParams(dimension_semantics=sem, vmem_limit_bytes=VMEM_LIMIT_BYTES)


def _layer_norm(x, g, b):
    mu = jnp.mean(x, axis=-1, keepdims=True)
    xc = x - mu
    var = jnp.mean(xc * xc, axis=-1, keepdims=True)
    return xc * lax.rsqrt(var + LN_EPS) * g + b


def _iota(shape, dim):
    return lax.broadcasted_iota(jnp.int32, shape, dim)


def _div(x, d):
    assert d & (d - 1) == 0
    return jnp.right_shift(x, d.bit_length() - 1)


def _s5_body(T, chain, u_ref, lre_ref, lim_ref, ldt_ref, btre_ref, btim_ref, cre_ref, cim_ref,
             dcol_ref, h0re_ref, h0im_ref, y_ref, hre_ref, him_ref, m_scr, *scan_scr):
    TC = T * SSM_GROUP
    dt = jnp.exp(ldt_ref[...])
    lr = lre_ref[...]
    li = lim_ref[...]
    mag = jnp.exp(lr * dt)
    abr = mag * jnp.cos(li * dt)
    abi = mag * jnp.sin(li * dt)
    nre = abr - 1.0
    den = lr * lr + li * li
    fre = (nre * lr + abi * li) / den
    fim = (abi * lr - nre * li) / den

    def powers(tau):
        pm = jnp.exp(lr * dt * tau)
        ang = li * dt * tau
        return pm * jnp.cos(ang), pm * jnp.sin(ang)

    tau = _iota((T, SSM_STATE), 0).astype(F32)
    pwr, pwi = powers(tau)
    rvr, rvi = powers((T - 1.0) - tau)
    pfr, pfi = pwr * fre - pwi * fim, pwr * fim + pwi * fre
    par, pai = pwr * abr - pwi * abi, pwr * abi + pwi * abr
    rfr, rfi = rvr * fre - rvi * fim, rvr * fim + rvi * fre

    rep_m = (jnp.right_shift(_iota((TC, T), 0), 4) == _iota((TC, T), 1)).astype(F32)
    til_m = (jnp.bitwise_and(_iota((TC, SSM_GROUP), 0), SSM_GROUP - 1) == _iota((TC, SSM_GROUP), 1)).astype(F32)
    rep = lambda a: jnp.dot(rep_m, a, precision=HI, preferred_element_type=F32)
    til = lambda a: jnp.dot(til_m, a, precision=HI, preferred_element_type=F32)
    c_re, c_im = til(cre_ref[...]), til(cim_ref[...])
    b_re, b_im = til(btre_ref[...]), til(btim_ref[...])

    pfr_r, pfi_r = rep(pfr), rep(pfi)
    caf_re = c_re * pfr_r - c_im * pfi_r
    caf_im = c_re * pfi_r + c_im * pfr_r
    kt = (lax.dot_general(btre_ref[...], caf_re, NT, precision=HI, preferred_element_type=F32)
          - lax.dot_general(btim_ref[...], caf_im, NT, precision=HI, preferred_element_type=F32))
    row = _iota((SSM_GROUP, TC), 0)
    lane = _iota((SSM_GROUP, TC), 1)
    kt = kt + jnp.where(lane == row, dcol_ref[...], 0.0)
    for s in range(T):
        shifted = pltpu.roll(kt, SSM_GROUP * s, axis=1) if s else kt
        m_scr[SSM_GROUP * s:SSM_GROUP * (s + 1), :] = jnp.where(lane >= SSM_GROUP * s, shifted, 0.0)

    u = u_ref[...]
    rfr_r, rfi_r = rep(rfr), rep(rfi)
    wm_re = rfr_r * b_re - rfi_r * b_im
    wm_im = rfr_r * b_im + rfi_r * b_re
    s_re = jnp.dot(u, wm_re, precision=HI, preferred_element_type=F32)
    s_im = jnp.dot(u, wm_im, precision=HI, preferred_element_type=F32)
    atr, ati = powers(float(T))

    if chain:
        sre_scr, sim_scr, hpre_scr, hpim_scr = scan_scr
        sre_scr[...] = s_re
        sim_scr[...] = s_im

        def step(c, carry):
            hr, hi = carry
            hpre_scr[pl.ds(c, 1), :] = hr
            hpim_scr[pl.ds(c, 1), :] = hi
            sr = sre_scr[pl.ds(c, 1), :]
            si = sim_scr[pl.ds(c, 1), :]
            return atr * hr - ati * hi + sr, atr * hi + ati * hr + si

        hr, hi = lax.fori_loop(0, u.shape[0], step, (h0re_ref[...], h0im_ref[...]))
        hre_ref[...] = hr
        him_ref[...] = hi
        hp_re, hp_im = hpre_scr[...], hpim_scr[...]
    else:
        hp_re, hp_im = h0re_ref[...], h0im_ref[...]
        hre_ref[...] = atr * hp_re - ati * hp_im + s_re
        him_ref[...] = atr * hp_im + ati * hp_re + s_im

    par_r, pai_r = rep(par), rep(pai)
    ca1_re = c_re * par_r - c_im * pai_r
    ca1_im = c_re * pai_r + c_im * par_r
    y = (jnp.dot(u, m_scr[...], precision=HI, preferred_element_type=F32)
         + lax.dot_general(hp_re, ca1_re, NT, precision=HI, preferred_element_type=F32)
         - lax.dot_general(hp_im, ca1_im, NT, precision=HI, preferred_element_type=F32))
    y_ref[...] = jax.nn.gelu(y)


def _s5_call(ug, h0re, h0im, lam_re, lam_im, log_dt, b_re, b_im, c_re, c_im, d_skip, T, chain):
    G, R, TC = ug.shape
    R0 = h0re.shape[1]
    P = SSM_STATE
    vec = lambda a: a.reshape(G, 1, -1)
    bt = lambda a: jnp.swapaxes(a, 1, 2)
    grp = lambda *shape: pl.BlockSpec((None,) + shape, lambda g: (g,) + (0,) * len(shape))
    scratch = [pltpu.VMEM((TC, TC), F32)]
    if chain:
        scratch += [pltpu.VMEM((R, P), F32)] * 4
    return pl.pallas_call(
        functools.partial(_s5_body, T, chain),
        grid=(G,),
        in_specs=[grp(R, TC), grp(1, P), grp(1, P), grp(1, 1), grp(SSM_GROUP, P), grp(SSM_GROUP, P),
                  grp(SSM_GROUP, P), grp(SSM_GROUP, P), grp(SSM_GROUP, 1), grp(R0, P), grp(R0, P)],
        out_specs=[grp(R, TC), grp(R0, P), grp(R0, P)],
        out_shape=[jax.ShapeDtypeStruct((G, R, TC), F32), jax.ShapeDtypeStruct((G, R0, P), F32),
                   jax.ShapeDtypeStruct((G, R0, P), F32)],
        scratch_shapes=scratch,
        compiler_params=_params("arbitrary"),
        name="s5_chain" if chain else "s5_step",
    )(ug, vec(lam_re), vec(lam_im), log_dt.reshape(G, 1, 1), bt(b_re), bt(b_im), c_re, c_im,
      d_skip.reshape(G, SSM_GROUP, 1), h0re, h0im)


def _glu_ln_body(y_ref, x_ref, w_ref, g_ref, b_ref, o_ref):
    z = jnp.dot(y_ref[...].astype(BF16), w_ref[...], preferred_element_type=F32)
    mix = z[:, :D_MODEL] * jax.nn.sigmoid(z[:, D_MODEL:])
    o_ref[...] = _layer_norm(DN_ALPHA * x_ref[...] + mix, g_ref[...], b_ref[...])


def _glu_ln_call(y, x, w_glu, g, b):
    n = x.shape[0]
    rows = pl.BlockSpec((ROW_BLOCK, D_MODEL), lambda i: (i, 0))
    full = lambda a: pl.BlockSpec(a.shape, lambda i: (0,) * a.ndim)
    w = w_glu.astype(BF16)
    g, b = g.reshape(1, -1), b.reshape(1, -1)
    return pl.pallas_call(
        _glu_ln_body, grid=(n // ROW_BLOCK,),
        in_specs=[rows, rows, full(w), full(g), full(b)], out_specs=rows,
        out_shape=jax.ShapeDtypeStruct((n, D_MODEL), F32),
        compiler_params=_params("arbitrary"), name="glu_ln",
    )(y, x, w, g, b)


def _top16_rows(s):
    ridx = _iota((PEER_TOPK, s.shape[1]), 0)

    def body(r, carry):
        work, out = carry
        m = jnp.max(work, axis=0, keepdims=True)
        out = jnp.where(ridx == r, m, out)
        return jnp.where(work == m, NEG, work), out

    _, out = lax.fori_loop(0, PEER_TOPK, body, (s, jnp.zeros((PEER_TOPK, s.shape[1]), F32)))
    return out


def _peer_body(x_ref, wq_ref, k1_ref, k2_ref, u_ref, v_ref, g_ref, b_ref, o_ref,
               xb_scr, q_scr, s1_scr, s2_scr, e1_scr, e2_scr, tau_scr, ux_scr, w_scr, acc_scr):
    e = pl.program_id(1)
    n_e = pl.num_programs(1)

    @pl.when(e == 0)
    def _select():
        xb = x_ref[...].astype(BF16)
        xb_scr[...] = xb
        q = jnp.dot(xb, wq_ref[...], preferred_element_type=F32)
        for j in range(2 * PEER_HEADS):
            q_scr[j] = q[:, PEER_HALF * j:PEER_HALF * (j + 1)]

        def head(h, carry):
            s1 = lax.dot_general(k1_ref[...], q_scr[2 * h], NT, precision=HI, preferred_element_type=F32)
            s2 = lax.dot_general(k2_ref[...], q_scr[2 * h + 1], NT, precision=HI, preferred_element_type=F32)
            a = _top16_rows(s1)
            b = _top16_rows(s2)
            cand = jnp.concatenate([a[i:i + 1, :] + b for i in range(PEER_TOPK)], axis=0)
            best = _top16_rows(cand)
            top = best[0:1, :]
            z = jnp.sum(jnp.exp(best - top), axis=0, keepdims=True)
            s1_scr[h] = s1
            s2_scr[h] = s2
            e1_scr[h] = jnp.exp(s1 - a[0:1, :]) / z
            e2_scr[h] = jnp.exp(s2 - b[0:1, :])
            tau_scr[h] = best[PEER_TOPK - 1:PEER_TOPK, :]
            return carry

        lax.fori_loop(0, PEER_HEADS, head, 0)
        acc_scr[...] = jnp.zeros_like(acc_scr)

    ux_scr[...] = lax.dot_general(u_ref[...], xb_scr[...], NT, preferred_element_type=F32)

    def per_i1(il, carry):
        i1 = e * PEER_I1_PER_STEP + il
        r0 = pl.multiple_of(il * PEER_NKEYS, PEER_NKEYS)
        gate = jnp.zeros((PEER_NKEYS, ux_scr.shape[1]), F32)
        for h in range(PEER_HEADS):
            t = s1_scr[h, pl.ds(i1, 1), :] + s2_scr[h]
            gate = gate + jnp.where(t >= tau_scr[h], e1_scr[h, pl.ds(i1, 1), :] * e2_scr[h], 0.0)
        w_scr[pl.ds(r0, PEER_NKEYS), :] = (gate * jax.nn.gelu(ux_scr[pl.ds(r0, PEER_NKEYS), :])).astype(BF16)
        return carry

    lax.fori_loop(0, PEER_I1_PER_STEP, per_i1, 0)
    acc_scr[...] += lax.dot_general(w_scr[...], v_ref[...], TN, preferred_element_type=F32)

    @pl.when(e == n_e - 1)
    def _finish():
        o_ref[...] = _layer_norm(DN_ALPHA * x_ref[...] + acc_scr[...], g_ref[...], b_ref[...])


def _peer_ln_call(x, w_pq, k1, k2, u_tab, v_tab, g, b):
    n = x.shape[0]
    nr = PEER_ROWS
    eb = PEER_I1_PER_STEP * PEER_NKEYS
    rows = pl.BlockSpec((nr, D_MODEL), lambda i, e: (i, 0))
    full = lambda a: pl.BlockSpec(a.shape, lambda i, e: (0,) * a.ndim)
    tab = pl.BlockSpec((eb, D_MODEL), lambda i, e: (e, 0))
    wq = w_pq.astype(BF16)
    g, b = g.reshape(1, -1), b.reshape(1, -1)
    per_head = pltpu.VMEM((PEER_HEADS, PEER_NKEYS, nr), F32)
    return pl.pallas_call(
        _peer_body, grid=(n // nr, PEER_NKEYS // PEER_I1_PER_STEP),
        in_specs=[rows, full(wq), full(k1), full(k2), tab, tab, full(g), full(b)], out_specs=rows,
        out_shape=jax.ShapeDtypeStruct((n, D_MODEL), F32),
        scratch_shapes=[pltpu.VMEM((nr, D_MODEL), BF16), pltpu.VMEM((2 * PEER_HEADS, nr, PEER_HALF), F32),
                        per_head, per_head, per_head, per_head, pltpu.VMEM((PEER_HEADS, 1, nr), F32),
                        pltpu.VMEM((eb, nr), F32), pltpu.VMEM((eb, nr), BF16), pltpu.VMEM((nr, D_MODEL), F32)],
        compiler_params=_params("arbitrary", "arbitrary"), name="peer_ln",
    )(x, wq, k1, k2, u_tab.astype(BF16), v_tab.astype(BF16), g, b)


def _kvq_body(x_ref, wkvq_ref, wf_ref, bf_ref, k_ref, v_ref, lf_ref, kb_ref, vb_ref, qb_ref):
    x = x_ref[...]
    z = jnp.dot(x.astype(BF16), wkvq_ref[...], preferred_element_type=F32)
    k = z[:, :D_MODEL]
    v = z[:, D_MODEL:2 * D_MODEL]
    k_ref[...] = k
    v_ref[...] = v
    kb_ref[...] = k.astype(BF16)
    vb_ref[...] = v.astype(BF16)
    qb_ref[...] = (z[:, 2 * D_MODEL:] * ATTN_SCALE).astype(BF16)
    zf = jnp.dot(x, wf_ref[...], precision=HI, preferred_element_type=F32)[:, :N_HEADS] + bf_ref[...]
    lf_ref[...] = jnp.minimum(zf, 0.0) - jnp.log(1.0 + jnp.exp(-jnp.abs(zf)))


def _kvq_call(x, w_kvf, b_f, w_q):
    n = x.shape[0]
    rows = pl.BlockSpec((ROW_BLOCK, D_MODEL), lambda i: (i, 0))
    rows_h = pl.BlockSpec((ROW_BLOCK, N_HEADS), lambda i: (i, 0))
    full = lambda a: pl.BlockSpec(a.shape, lambda i: (0,) * a.ndim)
    wkvq = jnp.concatenate([w_kvf[:, :2 * D_MODEL], w_q], axis=1).astype(BF16)
    wf = jnp.pad(w_kvf[:, 2 * D_MODEL:], ((0, 0), (0, 128 - N_HEADS)))
    bf = b_f.reshape(1, N_HEADS)
    f32 = jax.ShapeDtypeStruct((n, D_MODEL), F32)
    b16 = jax.ShapeDtypeStruct((n, D_MODEL), BF16)
    return pl.pallas_call(
        _kvq_body, grid=(n // ROW_BLOCK,),
        in_specs=[rows, full(wkvq), full(wf), full(bf)],
        out_specs=[rows, rows, rows_h, rows, rows, rows],
        out_shape=[f32, f32, jax.ShapeDtypeStruct((n, N_HEADS), F32), b16, b16, b16],
        compiler_params=_params("arbitrary"), name="kvq",
    )(x, wkvq, wf, bf)


def _cumsum_body(lf_ref, c_ref, carry_scr):
    @pl.when(pl.program_id(1) == 0)
    def _():
        carry_scr[...] = jnp.zeros_like(carry_scr)

    n = lf_ref.shape[0]
    tri = (_iota((n, n), 1) <= _iota((n, n), 0)).astype(F32)
    c = jnp.dot(tri, lf_ref[...], precision=HI, preferred_element_type=F32) + carry_scr[...]
    c_ref[...] = c
    carry_scr[...] = c[n - 1:n, :]


def _cumsum_call(lf, block):
    B, T, Hh = lf.shape
    spec = pl.BlockSpec((None, block, Hh), lambda b, i: (b, i, 0))
    return pl.pallas_call(
        _cumsum_body, grid=(B, T // block), in_specs=[spec], out_specs=spec,
        out_shape=jax.ShapeDtypeStruct(lf.shape, F32),
        scratch_shapes=[pltpu.VMEM((1, Hh), F32)],
        compiler_params=_params("arbitrary", "arbitrary"), name="cumsum_logf",
    )(lf)


def _fox_prompt_body(q_ref, k_ref, v_ref, cq_ref, ckt_ref, o_ref, m_scr, l_scr, acc_scr):
    i = pl.program_id(0)
    j = pl.program_id(1)
    tq, tk = q_ref.shape[0], k_ref.shape[0]

    @pl.when(j == 0)
    def _init():
        m_scr[...] = jnp.full_like(m_scr, NEG)
        l_scr[...] = jnp.zeros_like(l_scr)
        acc_scr[...] = jnp.zeros_like(acc_scr)

    @pl.when(j <= i)
    def _block():
        allowed = (j * tk + _iota((tq, tk), 1)) <= (i * tq + _iota((tq, tk), 0))
        for h in range(N_HEADS):
            sl = slice(HEAD_DIM * h, HEAD_DIM * (h + 1))
            s = lax.dot_general(q_ref[:, sl], k_ref[:, sl], NT, preferred_element_type=F32)
            s = s + cq_ref[:, h:h + 1] - ckt_ref[h:h + 1, :]
            s = jnp.where(allowed, s, NEG)
            m_prev = m_scr[h]
            m_new = jnp.maximum(m_prev, jnp.max(s, axis=1, keepdims=True))
            alpha = jnp.exp(m_prev - m_new)
            p = jnp.exp(s - m_new)
            l_scr[h] = alpha * l_scr[h] + jnp.sum(p, axis=1, keepdims=True)
            acc_scr[h] = alpha * acc_scr[h] + jnp.dot(p.astype(BF16), v_ref[:, sl], preferred_element_type=F32)
            m_scr[h] = m_new

    @pl.when(j == pl.num_programs(1) - 1)
    def _finish():
        for h in range(N_HEADS):
            o_ref[:, HEAD_DIM * h:HEAD_DIM * (h + 1)] = (acc_scr[h] / l_scr[h]).astype(o_ref.dtype)


def _fox_prompt_call(qb, kb, vb, c, ct):
    n = qb.shape[0]
    t = ATT_BLOCK
    nb = n // t
    qrow = pl.BlockSpec((t, D_MODEL), lambda i, j: (i, 0))
    krow = pl.BlockSpec((t, D_MODEL), lambda i, j: (jnp.minimum(i, j), 0))
    return pl.pallas_call(
        _fox_prompt_body, grid=(nb, nb),
        in_specs=[qrow, krow, krow, pl.BlockSpec((t, N_HEADS), lambda i, j: (i, 0)),
                  pl.BlockSpec((N_HEADS, t), lambda i, j: (0, jnp.minimum(i, j)))],
        out_specs=qrow,
        out_shape=jax.ShapeDtypeStruct((n, D_MODEL), BF16),
        scratch_shapes=[pltpu.VMEM((N_HEADS, t, 1), F32), pltpu.VMEM((N_HEADS, t, 1), F32),
                        pltpu.VMEM((N_HEADS, t, HEAD_DIM), F32)],
        compiler_params=_params("arbitrary", "arbitrary"), name="fox_prompt",
    )(qb, kb, vb, c, ct)


def _fox_sample_body(q_ref, kn_ref, vn_ref, kp_ref, vp_ref, cp_ref, cn_ref, cq_ref, o_ref):
    tq = q_ref.shape[0]
    hq = N_HEADS * tq
    q = q_ref[...]
    qexp = jnp.concatenate([q] * N_HEADS, axis=0)
    same_head = _div(_iota((hq, D_MODEL), 0), tq) == _div(_iota((hq, D_MODEL), 1), HEAD_DIM)
    qexp = jnp.where(same_head, qexp, jnp.zeros_like(qexp))
    expand = (_div(_iota((N_HEADS, hq), 1), tq) == _iota((N_HEADS, hq), 0)).astype(F32)
    cq = cq_ref[...]
    s_p = lax.dot_general(kp_ref[...].astype(BF16), qexp, NT, preferred_element_type=F32)
    s_p = s_p + cq - jnp.dot(cp_ref[...], expand, precision=HI, preferred_element_type=F32)
    s_n = lax.dot_general(kn_ref[...], qexp, NT, preferred_element_type=F32)
    s_n = s_n + cq - jnp.dot(cn_ref[...], expand, precision=HI, preferred_element_type=F32)
    causal = _iota((tq, hq), 0) <= jnp.bitwise_and(_iota((tq, hq), 1), tq - 1)
    s_n = jnp.where(causal, s_n, NEG)
    m = jnp.maximum(jnp.max(s_p, axis=0, keepdims=True), jnp.max(s_n, axis=0, keepdims=True))
    p_p = jnp.exp(s_p - m)
    p_n = jnp.exp(s_n - m)
    inv = 1.0 / (jnp.sum(p_p, axis=0, keepdims=True) + jnp.sum(p_n, axis=0, keepdims=True))
    o_all = (lax.dot_general((p_p * inv).astype(BF16), vp_ref[...].astype(BF16), TN, preferred_element_type=F32)
             + lax.dot_general((p_n * inv).astype(BF16), vn_ref[...], TN, preferred_element_type=F32))
    lane_head = _div(_iota((tq, D_MODEL), 1), HEAD_DIM)
    out = jnp.zeros((tq, D_MODEL), F32)
    for h in range(N_HEADS):
        out = out + jnp.where(lane_head == h, o_all[tq * h:tq * (h + 1), :], 0.0)
    o_ref[...] = out.astype(o_ref.dtype)


def _fox_sample_call(qb, kb, vb, row0, cache_k, cache_v, c_all):
    B, Tp = cache_k.shape[0], cache_k.shape[1]
    tq = c_all.shape[1] - Tp
    blk0 = row0 // tq
    new = pl.BlockSpec((tq, D_MODEL), lambda b: (blk0 + b, 0))
    past = pl.BlockSpec((None, Tp, D_MODEL), lambda b: (b, 0, 0))
    c_past, c_new = c_all[:, :Tp], c_all[:, Tp:]
    cq = jnp.swapaxes(c_new, 1, 2).reshape(B, 1, N_HEADS * tq)
    return pl.pallas_call(
        _fox_sample_body, grid=(B,),
        in_specs=[new, new, new, past, past, pl.BlockSpec((None, Tp, N_HEADS), lambda b: (b, 0, 0)),
                  pl.BlockSpec((None, tq, N_HEADS), lambda b: (b, 0, 0)),
                  pl.BlockSpec((None, 1, N_HEADS * tq), lambda b: (b, 0, 0))],
        out_specs=pl.BlockSpec((tq, D_MODEL), lambda b: (b, 0)),
        out_shape=jax.ShapeDtypeStruct((B * tq, D_MODEL), BF16),
        compiler_params=_params("arbitrary"), name="fox_sample",
    )(qb, kb, vb, cache_k.reshape(B, Tp, D_MODEL), cache_v.reshape(B, Tp, D_MODEL), c_past, c_new, cq)


def _oproj_ln_body(o_ref, x_ref, w_ref, g_ref, b_ref, y_ref):
    mix = jnp.dot(o_ref[...], w_ref[...], preferred_element_type=F32)
    y_ref[...] = _layer_norm(DN_ALPHA * x_ref[...] + mix, g_ref[...], b_ref[...])


def _oproj_ln_call(o, x, w_o, g, b):
    n = x.shape[0]
    rows = pl.BlockSpec((ROW_BLOCK, D_MODEL), lambda i: (i, 0))
    full = lambda a: pl.BlockSpec(a.shape, lambda i: (0,) * a.ndim)
    w = w_o.astype(BF16)
    g, b = g.reshape(1, -1), b.reshape(1, -1)
    return pl.pallas_call(
        _oproj_ln_body, grid=(n // ROW_BLOCK,),
        in_specs=[rows, rows, full(w), full(g), full(b)], out_specs=rows,
        out_shape=jax.ShapeDtypeStruct((n, D_MODEL), F32),
        compiler_params=_params("arbitrary"), name="oproj_ln",
    )(o, x, w, g, b)


def _group_major(x, T):
    R = x.shape[0] // T
    return x.reshape(R, T, SSM_GROUPS, SSM_GROUP).transpose(2, 0, 1, 3).reshape(SSM_GROUPS, R, T * SSM_GROUP)


def _row_major(yg, T):
    G, R, _ = yg.shape
    return yg.reshape(G, R, T, SSM_GROUP).transpose(1, 2, 0, 3).reshape(R * T, D_MODEL)


def kernel(x_prompt, x_sample, state_ssm_re, state_ssm_im, cache_k, cache_v, cache_logf, meta_tokens, ssm_lam_re, ssm_lam_im, ssm_log_dt, ssm_b_re, ssm_b_im, ssm_c_re, ssm_c_im, ssm_d, w_glu, w_kvf, b_f, w_q, w_o, peer_w_q, peer_k1, peer_k2, peer_u, peer_v, ln1_g, ln1_b, ln2_g, ln2_b):
    bp, seq, _ = x_prompt.shape
    bs, ts, _ = x_sample.shape
    assert bp == 1, "the token stream holds one new stream"
    n_p = N_META + seq
    n_s = bs * ts
    n_rows = -(-(n_p + n_s) // ROW_BLOCK) * ROW_BLOCK
    assert n_p % ts == 0 and n_rows % PEER_ROWS == 0 and n_rows % ATT_BLOCK == 0

    xp = jnp.concatenate([meta_tokens.astype(F32), x_prompt[0]], axis=0)
    xs = x_sample.reshape(n_s, D_MODEL)
    x0 = jnp.concatenate([xp, xs, jnp.zeros((n_rows - n_p - n_s, D_MODEL), F32)], axis=0)

    ssm = (ssm_lam_re[0], ssm_lam_im[0], ssm_log_dt[0], ssm_b_re[0], ssm_b_im[0], ssm_c_re[0], ssm_c_im[0], ssm_d[0])
    n_chunks = -(-n_p // (CHUNK * 8)) * 8
    pad = n_chunks * CHUNK - n_p
    ug_p = _group_major(jnp.pad(xp, ((pad, 0), (0, 0))), CHUNK)
    zero_h = jnp.zeros((SSM_GROUPS, 1, SSM_STATE), F32)
    yg_p, hre_p, him_p = _s5_call(ug_p, zero_h, zero_h, *ssm, T=CHUNK, chain=True)
    y_p = _row_major(yg_p, CHUNK)[pad:]
    ug_s = x_sample.reshape(bs, ts, SSM_GROUPS, SSM_GROUP).transpose(2, 0, 1, 3).reshape(SSM_GROUPS, bs, ts * SSM_GROUP)
    h0re = jnp.swapaxes(state_ssm_re[:, 0], 0, 1)
    h0im = jnp.swapaxes(state_ssm_im[:, 0], 0, 1)
    yg_s, hre_s, him_s = _s5_call(ug_s, h0re, h0im, *ssm, T=ts, chain=False)
    y_s = yg_s.reshape(SSM_GROUPS, bs, ts, SSM_GROUP).transpose(1, 2, 0, 3).reshape(n_s, D_MODEL)
    y0 = jnp.concatenate([y_p, y_s, jnp.zeros((n_rows - n_p - n_s, D_MODEL), F32)], axis=0)

    x1 = _glu_ln_call(y0, x0, w_glu[0], ln1_g[0], ln1_b[0])
    x2 = _peer_ln_call(x1, peer_w_q[0], peer_k1[0], peer_k2[0], peer_u[0], peer_v[0], ln2_g[0], ln2_b[0])

    k_all, v_all, lf_all, kb, vb, qb = _kvq_call(x2, w_kvf, b_f, w_q[0])
    c_p = _cumsum_call(lf_all.reshape(1, n_rows, N_HEADS), ROW_BLOCK)[0]
    o_p = _fox_prompt_call(qb, kb, vb, c_p, c_p.T)
    lf_s = lf_all[n_p:n_p + n_s].reshape(bs, ts, N_HEADS)
    lf_cat = jnp.concatenate([cache_logf.astype(F32), lf_s], axis=1)
    c_s = _cumsum_call(lf_cat, lf_cat.shape[1] // 2)
    o_s = _fox_sample_call(qb, kb, vb, n_p, cache_k, cache_v, c_s)
    o_all = lax.dynamic_update_slice(o_p, o_s, (n_p, 0))
    x3 = _oproj_ln_call(o_all, x2, w_o[0], ln1_g[1], ln1_b[1])
    x4 = _peer_ln_call(x3, peer_w_q[1], peer_k1[1], peer_k2[1], peer_u[1], peer_v[1], ln2_g[1], ln2_b[1])

    heads = lambda a, b, t: a.reshape(b, t, N_HEADS, HEAD_DIM)
    y_prompt = x4[N_META:n_p].reshape(bp, seq, D_MODEL)
    y_sample = x4[n_p:n_p + n_s].reshape(bs, ts, D_MODEL)
    st_p = lambda h: h.reshape(1, 1, SSM_GROUPS, SSM_STATE)
    st_s = lambda h: jnp.swapaxes(h, 0, 1).reshape(bs, 1, SSM_GROUPS, SSM_STATE)
    return (y_prompt, y_sample, st_p(hre_p), st_p(him_p),
            heads(k_all[:n_p], bp, n_p), heads(v_all[:n_p], bp, n_p), lf_all[:n_p].reshape(bp, n_p, N_HEADS),
            st_s(hre_s), st_s(him_s),
            heads(k_all[n_p:n_p + n_s], bs, ts), heads(v_all[n_p:n_p + n_s], bs, ts), lf_s)
```

```python
import functools
import math

import jax
import jax.numpy as jnp
from jax import lax
from jax.experimental import pallas as pl
from jax.experimental.pallas import tpu as pltpu

D_MODEL = 1024
N_META = 16
CHUNK = 64
SSM_GROUP = 16
SSM_GROUPS = D_MODEL // SSM_GROUP
SSM_STATE = 64
N_HEADS = 16
HEAD_DIM = D_MODEL // N_HEADS
ATTN_SCALE = HEAD_DIM ** -0.5
PEER_HEADS = 8
PEER_NKEYS = 128
PEER_TOPK = 16
PEER_DKEY = 256
PEER_HALF = PEER_DKEY // 2
LN_EPS = 1e-5
DEPTH = 2
DN_ALPHA = (2.0 * DEPTH) ** 0.25

VMEM_LIMIT_BYTES = 56 * 1024 * 1024
ROW_BLOCK = 512
PEER_ROWS = 512
PEER_I1_PER_STEP = 8
ATT_BLOCK = 512
TILE_ROWS = 256
HEAD_TILE = 128
BIAS_PIECES = 3
NEG = -1e30

F32 = jnp.float32
BF16 = jnp.bfloat16
HI = lax.Precision.HIGHEST
NT = (((1,), (1,)), ((), ()))
TN = (((0,), (0,)), ((), ()))


def _params(*sem):
    return pltpu.CompilerParams(dimension_semantics=sem, vmem_limit_bytes=VMEM_LIMIT_BYTES)


def _layer_norm(x, g, b):
    mu = jnp.mean(x, axis=-1, keepdims=True)
    xc = x - mu
    var = jnp.mean(xc * xc, axis=-1, keepdims=True)
    return xc * lax.rsqrt(var + LN_EPS) * g + b


def _iota(shape, dim):
    return lax.broadcasted_iota(jnp.int32, shape, dim)


def _div(x, d):
    assert d & (d - 1) == 0
    return jnp.right_shift(x, d.bit_length() - 1)


def _s5_body(T, chain, u_ref, lre_ref, lim_ref, ldt_ref, btre_ref, btim_ref, cre_ref, cim_ref,
             dcol_ref, h0re_ref, h0im_ref, y_ref, hre_ref, him_ref, m_scr, *scan_scr):
    TC = T * SSM_GROUP
    dt = jnp.exp(ldt_ref[...])
    lr = lre_ref[...]
    li = lim_ref[...]
    mag = jnp.exp(lr * dt)
    abr = mag * jnp.cos(li * dt)
    abi = mag * jnp.sin(li * dt)
    nre = abr - 1.0
    den = lr * lr + li * li
    fre = (nre * lr + abi * li) / den
    fim = (abi * lr - nre * li) / den

    def powers(tau):
        pm = jnp.exp(lr * dt * tau)
        ang = li * dt * tau
        return pm * jnp.cos(ang), pm * jnp.sin(ang)

    tau = _iota((T, SSM_STATE), 0).astype(F32)
    pwr, pwi = powers(tau)
    rvr, rvi = powers((T - 1.0) - tau)
    pfr, pfi = pwr * fre - pwi * fim, pwr * fim + pwi * fre
    par, pai = pwr * abr - pwi * abi, pwr * abi + pwi * abr
    rfr, rfi = rvr * fre - rvi * fim, rvr * fim + rvi * fre

    rep_m = (jnp.right_shift(_iota((TC, T), 0), 4) == _iota((TC, T), 1)).astype(F32)
    til_m = (jnp.bitwise_and(_iota((TC, SSM_GROUP), 0), SSM_GROUP - 1) == _iota((TC, SSM_GROUP), 1)).astype(F32)
    rep = lambda a: jnp.dot(rep_m, a, precision=HI, preferred_element_type=F32)
    til = lambda a: jnp.dot(til_m, a, precision=HI, preferred_element_type=F32)
    c_re, c_im = til(cre_ref[...]), til(cim_ref[...])
    b_re, b_im = til(btre_ref[...]), til(btim_ref[...])

    pfr_r, pfi_r = rep(pfr), rep(pfi)
    caf_re = c_re * pfr_r - c_im * pfi_r
    caf_im = c_re * pfi_r + c_im * pfr_r
    kt = (lax.dot_general(btre_ref[...], caf_re, NT, precision=HI, preferred_element_type=F32)
          - lax.dot_general(btim_ref[...], caf_im, NT, precision=HI, preferred_element_type=F32))
    row = _iota((SSM_GROUP, TC), 0)
    lane = _iota((SSM_GROUP, TC), 1)
    kt = kt + jnp.where(lane == row, dcol_ref[...], 0.0)
    for s in range(T):
        shifted = pltpu.roll(kt, SSM_GROUP * s, axis=1) if s else kt
        m_scr[SSM_GROUP * s:SSM_GROUP * (s + 1), :] = jnp.where(lane >= SSM_GROUP * s, shifted, 0.0)

    u = u_ref[...]
    rfr_r, rfi_r = rep(rfr), rep(rfi)
    wm_re = rfr_r * b_re - rfi_r * b_im
    wm_im = rfr_r * b_im + rfi_r * b_re
    s_re = jnp.dot(u, wm_re, precision=HI, preferred_element_type=F32)
    s_im = jnp.dot(u, wm_im, precision=HI, preferred_element_type=F32)
    atr, ati = powers(float(T))

    if chain:
        sre_scr, sim_scr, hpre_scr, hpim_scr = scan_scr
        sre_scr[...] = s_re
        sim_scr[...] = s_im

        def step(c, carry):
            hr, hi = carry
            hpre_scr[pl.ds(c, 1), :] = hr
            hpim_scr[pl.ds(c, 1), :] = hi
            sr = sre_scr[pl.ds(c, 1), :]
            si = sim_scr[pl.ds(c, 1), :]
            return atr * hr - ati * hi + sr, atr * hi + ati * hr + si

        hr, hi = lax.fori_loop(0, u.shape[0], step, (h0re_ref[...], h0im_ref[...]))
        hre_ref[...] = hr
        him_ref[...] = hi
        hp_re, hp_im = hpre_scr[...], hpim_scr[...]
    else:
        hp_re, hp_im = h0re_ref[...], h0im_ref[...]
        hre_ref[...] = atr * hp_re - ati * hp_im + s_re
        him_ref[...] = atr * hp_im + ati * hp_re + s_im

    par_r, pai_r = rep(par), rep(pai)
    ca1_re = c_re * par_r - c_im * pai_r
    ca1_im = c_re * pai_r + c_im * par_r
    y = (jnp.dot(u, m_scr[...], precision=HI, preferred_element_type=F32)
         + lax.dot_general(hp_re, ca1_re, NT, precision=HI, preferred_element_type=F32)
         - lax.dot_general(hp_im, ca1_im, NT, precision=HI, preferred_element_type=F32))
    y_ref[...] = jax.nn.gelu(y)


def _s5_call(ug, h0re, h0im, lam_re, lam_im, log_dt, b_re, b_im, c_re, c_im, d_skip, T, chain):
    G, R, TC = ug.shape
    R0 = h0re.shape[1]
    P = SSM_STATE
    vec = lambda a: a.reshape(G, 1, -1)
    bt = lambda a: jnp.swapaxes(a, 1, 2)
    grp = lambda *shape: pl.BlockSpec((None,) + shape, lambda g: (g,) + (0,) * len(shape))
    scratch = [pltpu.VMEM((TC, TC), F32)]
    if chain:
        scratch += [pltpu.VMEM((R, P), F32)] * 4
    return pl.pallas_call(
        functools.partial(_s5_body, T, chain),
        grid=(G,),
        in_specs=[grp(R, TC), grp(1, P), grp(1, P), grp(1, 1), grp(SSM_GROUP, P), grp(SSM_GROUP, P),
                  grp(SSM_GROUP, P), grp(SSM_GROUP, P), grp(SSM_GROUP, 1), grp(R0, P), grp(R0, P)],
        out_specs=[grp(R, TC), grp(R0, P), grp(R0, P)],
        out_shape=[jax.ShapeDtypeStruct((G, R, TC), F32), jax.ShapeDtypeStruct((G, R0, P), F32),
                   jax.ShapeDtypeStruct((G, R0, P), F32)],
        scratch_shapes=scratch,
        compiler_params=_params("arbitrary"),
        name="s5_chain" if chain else "s5_step",
    )(ug, vec(lam_re), vec(lam_im), log_dt.reshape(G, 1, 1), bt(b_re), bt(b_im), c_re, c_im,
      d_skip.reshape(G, SSM_GROUP, 1), h0re, h0im)


def _glu_ln_body(y_ref, x_ref, w_ref, g_ref, b_ref, o_ref):
    z = jnp.dot(y_ref[...].astype(BF16), w_ref[...], preferred_element_type=F32)
    mix = z[:, :D_MODEL] * jax.nn.sigmoid(z[:, D_MODEL:])
    o_ref[...] = _layer_norm(DN_ALPHA * x_ref[...] + mix, g_ref[...], b_ref[...])


def _glu_ln_call(y, x, w_glu, g, b):
    n = x.shape[0]
    rows = pl.BlockSpec((ROW_BLOCK, D_MODEL), lambda i: (i, 0))
    full = lambda a: pl.BlockSpec(a.shape, lambda i: (0,) * a.ndim)
    w = w_glu.astype(BF16)
    g, b = g.reshape(1, -1), b.reshape(1, -1)
    return pl.pallas_call(
        _glu_ln_body, grid=(n // ROW_BLOCK,),
        in_specs=[rows, rows, full(w), full(g), full(b)], out_specs=rows,
        out_shape=jax.ShapeDtypeStruct((n, D_MODEL), F32),
        compiler_params=_params("arbitrary"), name="glu_ln",
    )(y, x, w, g, b)


PEER_RANKS = PEER_TOPK + 1
PEER_RANK_ROWS = 24


def _top_rows(s):
    ridx = _iota((PEER_RANK_ROWS, s.shape[1]), 0)

    def body(r, carry):
        work, out = carry
        m = jnp.max(work, axis=0, keepdims=True)
        out = jnp.where(ridx == r, m, out)
        return jnp.where(work == m, NEG, work), out

    _, out = lax.fori_loop(0, PEER_RANKS, body, (s, jnp.full((PEER_RANK_ROWS, s.shape[1]), NEG, F32)))
    return out


def _pair_candidates(a, b):
    n = a.shape[1]
    rows8 = _iota((8, n), 0)
    parts = [a[0:1, :] + b]
    for i in range(1, 8):
        part = a[i:i + 1, :] + b[0:8, :]
        keep = PEER_RANKS // (i + 1)
        parts.append(part if keep >= 8 else jnp.where(rows8 < keep, part, NEG))
    parts.append(a[8:PEER_RANK_ROWS, :] + b[0:1, :])
    return jnp.concatenate(parts, axis=0)


def _peer_body(x_ref, wq_ref, k1_ref, k2_ref, u_ref, v_ref, g_ref, b_ref, o_ref,
               xb_scr, q_scr, s1_scr, s2_scr, e1_scr, e2_scr, tau_scr, acc_scr):
    e = pl.program_id(1)
    n_e = pl.num_programs(1)

    @pl.when(e == 0)
    def _select():
        xb = x_ref[...].astype(BF16)
        xb_scr[...] = xb
        q = jnp.dot(xb, wq_ref[...], preferred_element_type=F32)
        for j in range(2 * PEER_HEADS):
            q_scr[j] = q[:, PEER_HALF * j:PEER_HALF * (j + 1)]

        def head(h, carry):
            s1 = lax.dot_general(k1_ref[...], q_scr[2 * h], NT, precision=HI, preferred_element_type=F32)
            s2 = lax.dot_general(k2_ref[...], q_scr[2 * h + 1], NT, precision=HI, preferred_element_type=F32)
            a = _top_rows(s1)
            b = _top_rows(s2)
            best = _top_rows(_pair_candidates(a, b))
            top = best[0:1, :]
            chosen = _iota(best.shape, 0) < PEER_TOPK
            z = jnp.sum(jnp.where(chosen, jnp.exp(best - top), 0.0), axis=0, keepdims=True)
            s1_scr[h] = s1
            s2_scr[h] = s2
            e1_scr[h] = jnp.exp(s1 - a[0:1, :]) / z
            e2_scr[h] = jnp.exp(s2 - b[0:1, :])
            tau_scr[h] = 0.5 * (best[PEER_TOPK - 1:PEER_TOPK, :] + best[PEER_TOPK:PEER_TOPK + 1, :])
            return carry

        lax.fori_loop(0, PEER_HEADS, head, 0)
        acc_scr[...] = jnp.zeros_like(acc_scr)

    ux = lax.dot_general(u_ref[...], xb_scr[...], NT, preferred_element_type=F32)
    w_rows = []
    for il in range(PEER_I1_PER_STEP):
        i1 = e * PEER_I1_PER_STEP + il
        gate = None
        for h in range(PEER_HEADS):
            theta = tau_scr[h] - s1_scr[h, pl.ds(i1, 1), :]
            term = jnp.where(s2_scr[h] >= theta, e2_scr[h], 0.0) * e1_scr[h, pl.ds(i1, 1), :]
            gate = term if gate is None else gate + term
        act = jax.nn.gelu(ux[il * PEER_NKEYS:(il + 1) * PEER_NKEYS, :])
        w_rows.append((gate * act).astype(BF16))
    w = jnp.concatenate(w_rows, axis=0)
    acc_scr[...] += lax.dot_general(w, v_ref[...], TN, preferred_element_type=F32)

    @pl.when(e == n_e - 1)
    def _finish():
        o_ref[...] = _layer_norm(DN_ALPHA * x_ref[...] + acc_scr[...], g_ref[...], b_ref[...])


def _peer_ln_call(x, w_pq, k1, k2, u_tab, v_tab, g, b):
    n = x.shape[0]
    nr = PEER_ROWS
    eb = PEER_I1_PER_STEP * PEER_NKEYS
    rows = pl.BlockSpec((nr, D_MODEL), lambda i, e: (i, 0))
    full = lambda a: pl.BlockSpec(a.shape, lambda i, e: (0,) * a.ndim)
    tab = pl.BlockSpec((eb, D_MODEL), lambda i, e: (e, 0))
    wq = w_pq.astype(BF16)
    g, b = g.reshape(1, -1), b.reshape(1, -1)
    per_head = pltpu.VMEM((PEER_HEADS, PEER_NKEYS, nr), F32)
    return pl.pallas_call(
        _peer_body, grid=(n // nr, PEER_NKEYS // PEER_I1_PER_STEP),
        in_specs=[rows, full(wq), full(k1), full(k2), tab, tab, full(g), full(b)], out_specs=rows,
        out_shape=jax.ShapeDtypeStruct((n, D_MODEL), F32),
        scratch_shapes=[pltpu.VMEM((nr, D_MODEL), BF16), pltpu.VMEM((2 * PEER_HEADS, nr, PEER_HALF), F32),
                        per_head, per_head, per_head, per_head, pltpu.VMEM((PEER_HEADS, 1, nr), F32),
                        pltpu.VMEM((nr, D_MODEL), F32)],
        compiler_params=_params("arbitrary", "arbitrary"), name="peer_ln",
    )(x, wq, k1, k2, u_tab.astype(BF16), v_tab.astype(BF16), g, b)


def _kvq_body(x_ref, wkvq_ref, wf_ref, bf_ref, k_ref, v_ref, lf_ref, kb_ref, vb_ref, qb_ref):
    x = x_ref[...]
    z = jnp.dot(x.astype(BF16), wkvq_ref[...], preferred_element_type=F32)
    k = z[:, :D_MODEL]
    v = z[:, D_MODEL:2 * D_MODEL]
    k_ref[...] = k
    v_ref[...] = v
    kb_ref[...] = k.astype(BF16)
    vb_ref[...] = v.astype(BF16)
    qb_ref[...] = (z[:, 2 * D_MODEL:] * ATTN_SCALE).astype(BF16)
    zf = jnp.dot(x, wf_ref[...], precision=HI, preferred_element_type=F32)[:, :N_HEADS] + bf_ref[...]
    lf_ref[...] = jnp.minimum(zf, 0.0) - jnp.log(1.0 + jnp.exp(-jnp.abs(zf)))


def _kvq_call(x, w_kvf, b_f, w_q):
    n = x.shape[0]
    rows = pl.BlockSpec((ROW_BLOCK, D_MODEL), lambda i: (i, 0))
    rows_h = pl.BlockSpec((ROW_BLOCK, N_HEADS), lambda i: (i, 0))
    full = lambda a: pl.BlockSpec(a.shape, lambda i: (0,) * a.ndim)
    wkvq = jnp.concatenate([w_kvf[:, :2 * D_MODEL], w_q], axis=1).astype(BF16)
    wf = jnp.pad(w_kvf[:, 2 * D_MODEL:], ((0, 0), (0, 128 - N_HEADS)))
    bf = b_f.reshape(1, N_HEADS)
    f32 = jax.ShapeDtypeStruct((n, D_MODEL), F32)
    b16 = jax.ShapeDtypeStruct((n, D_MODEL), BF16)
    return pl.pallas_call(
        _kvq_body, grid=(n // ROW_BLOCK,),
        in_specs=[rows, full(wkvq), full(wf), full(bf)],
        out_specs=[rows, rows, rows_h, rows, rows, rows],
        out_shape=[f32, f32, jax.ShapeDtypeStruct((n, N_HEADS), F32), b16, b16, b16],
        compiler_params=_params("arbitrary"), name="kvq",
    )(x, wkvq, wf, bf)


def _cumsum_body(lf_ref, c_ref, carry_scr):
    @pl.when(pl.program_id(1) == 0)
    def _():
        carry_scr[...] = jnp.zeros_like(carry_scr)

    n = lf_ref.shape[0]
    tri = (_iota((n, n), 1) <= _iota((n, n), 0)).astype(F32)
    c = jnp.dot(tri, lf_ref[...], precision=HI, preferred_element_type=F32) + carry_scr[...]
    c_ref[...] = c
    carry_scr[...] = c[n - 1:n, :]


def _cumsum_call(lf, block):
    B, T, Hh = lf.shape
    spec = pl.BlockSpec((None, block, Hh), lambda b, i: (b, i, 0))
    return pl.pallas_call(
        _cumsum_body, grid=(B, T // block), in_specs=[spec], out_specs=spec,
        out_shape=jax.ShapeDtypeStruct(lf.shape, F32),
        scratch_shapes=[pltpu.VMEM((1, Hh), F32)],
        compiler_params=_params("arbitrary", "arbitrary"), name="cumsum_logf",
    )(lf)


def _head_tiles(w, axis):
    shape = w.shape[:axis] + (N_HEADS, HEAD_DIM) + w.shape[axis + 1:]
    pad = [(0, 0)] * (len(shape))
    pad[axis + 1] = (0, HEAD_TILE - HEAD_DIM)
    out = jnp.pad(w.reshape(shape), pad)
    return out.reshape(w.shape[:axis] + (N_HEADS * HEAD_TILE,) + w.shape[axis + 1:])


def _tiles_body(x_ref, lf_ref, w_ref, sq_ref, sk_ref, oq_ref, ok_ref, ov_ref, qa_ref, ka_ref, va_ref, carry_scr):
    @pl.when(pl.program_id(0) == 0)
    def _():
        carry_scr[...] = jnp.zeros_like(carry_scr)

    n = x_ref.shape[0]
    wt = N_HEADS * HEAD_TILE
    tri = (_iota((n, n), 1) <= _iota((n, n), 0)).astype(F32)
    c = jnp.dot(tri, lf_ref[...], precision=HI, preferred_element_type=F32) + carry_scr[...]
    carry_scr[...] = c[n - 1:n, :]
    hi = c.astype(BF16)
    rest = c - hi.astype(F32)
    mid = rest.astype(BF16)
    lo = (rest - mid.astype(F32)).astype(BF16)
    pieces = (hi, mid, lo)
    q_bias = sum(jnp.dot(p, sq_ref[i], preferred_element_type=F32) for i, p in enumerate(pieces))
    k_bias = sum(jnp.dot(p, sk_ref[i], preferred_element_type=F32) for i, p in enumerate(pieces))
    z = jnp.dot(x_ref[...].astype(BF16), w_ref[...], preferred_element_type=F32)
    qa_ref[...] = (z[:, :wt] * ATTN_SCALE + q_bias + oq_ref[...]).astype(BF16)
    ka_ref[...] = (z[:, wt:2 * wt] - k_bias + ok_ref[...]).astype(BF16)
    va_ref[...] = (z[:, 2 * wt:] + ov_ref[...]).astype(BF16)


def _tiles_call(x, lf, w_q, w_k, w_v):
    n = x.shape[0]
    wt = N_HEADS * HEAD_TILE
    nr = TILE_ROWS
    w = jnp.concatenate([_head_tiles(w_q, 1), _head_tiles(w_k, 1), _head_tiles(w_v, 1)], axis=1).astype(BF16)
    lane = jnp.arange(wt)
    head, off = lane // HEAD_TILE, lane % HEAD_TILE
    at = lambda o: ((head[None, None, :] == jnp.arange(N_HEADS)[None, :, None])
                    & (off[None, None, :] == o + jnp.arange(BIAS_PIECES)[:, None, None])).astype(BF16)
    sq, sk = at(HEAD_DIM), at(HEAD_DIM + BIAS_PIECES)
    ones = lambda lo, hi: ((off >= lo) & (off < hi)).astype(F32).reshape(1, wt)
    oq = ones(HEAD_DIM + BIAS_PIECES, HEAD_DIM + 2 * BIAS_PIECES)
    ok = ones(HEAD_DIM, HEAD_DIM + BIAS_PIECES)
    ov = ones(HEAD_DIM, HEAD_DIM + 1)
    full = lambda a: pl.BlockSpec(a.shape, lambda i: (0,) * a.ndim)
    tiles = pl.BlockSpec((nr, wt), lambda i: (i, 0))
    out = jax.ShapeDtypeStruct((n, wt), BF16)
    return pl.pallas_call(
        _tiles_body, grid=(n // nr,),
        in_specs=[pl.BlockSpec((nr, D_MODEL), lambda i: (i, 0)), pl.BlockSpec((nr, N_HEADS), lambda i: (i, 0)),
                  full(w), full(sq), full(sk), full(oq), full(ok), full(ov)],
        out_specs=[tiles, tiles, tiles], out_shape=[out, out, out],
        scratch_shapes=[pltpu.VMEM((1, N_HEADS), F32)],
        compiler_params=_params("arbitrary"), name="qkv_tiles",
    )(x, lf, w, sq, sk, oq, ok, ov)


def _fox_prompt_body(q_ref, k_ref, v_ref, o_ref, m_scr, acc_scr):
    i = pl.program_id(0)
    j = pl.program_id(1)
    tq, tk = q_ref.shape[0], k_ref.shape[0]

    @pl.when(j == 0)
    def _init():
        m_scr[...] = jnp.full_like(m_scr, NEG)
        acc_scr[...] = jnp.zeros_like(acc_scr)

    def block(diagonal):
        for h in range(N_HEADS):
            sl = slice(HEAD_TILE * h, HEAD_TILE * (h + 1))
            s = lax.dot_general(q_ref[:, sl], k_ref[:, sl], NT, preferred_element_type=F32)
            if diagonal:
                s = jnp.where(_iota((tq, tk), 1) <= _iota((tq, tk), 0), s, NEG)
            m_prev = m_scr[h]
            m_new = jnp.maximum(m_prev, jnp.max(s, axis=1, keepdims=True))
            p = jnp.exp(s - jnp.tile(m_new, (1, tk // HEAD_TILE))).astype(BF16)
            acc_scr[h] = (jnp.exp(m_prev - m_new) * acc_scr[h]
                          + jnp.dot(p, v_ref[:, sl], preferred_element_type=F32))
            m_scr[h] = m_new

    pl.when(j < i)(lambda: block(False))

    @pl.when(j == i)
    def _last():
        block(True)
        for h in range(N_HEADS):
            acc = acc_scr[h]
            o_ref[:, HEAD_TILE * h:HEAD_TILE * (h + 1)] = (acc / acc[:, HEAD_DIM:HEAD_DIM + 1]).astype(o_ref.dtype)


def _fox_prompt_call(qa, ka, va):
    n, wt = qa.shape
    t = ATT_BLOCK
    nb = n // t
    qrow = pl.BlockSpec((t, wt), lambda i, j: (i, 0))
    krow = pl.BlockSpec((t, wt), lambda i, j: (jnp.minimum(i, j), 0))
    return pl.pallas_call(
        _fox_prompt_body, grid=(nb, nb),
        in_specs=[qrow, krow, krow], out_specs=qrow,
        out_shape=jax.ShapeDtypeStruct((n, wt), BF16),
        scratch_shapes=[pltpu.VMEM((N_HEADS, t, HEAD_TILE), F32), pltpu.VMEM((N_HEADS, t, HEAD_TILE), F32)],
        compiler_params=_params("arbitrary", "arbitrary"), name="fox_prompt",
    )(qa, ka, va)


def _fox_sample_body(q_ref, kn_ref, vn_ref, kp_ref, vp_ref, cp_ref, cn_ref, cq_ref, o_ref):
    tq = q_ref.shape[0]
    hq = N_HEADS * tq
    q = q_ref[...]
    qexp = jnp.concatenate([q] * N_HEADS, axis=0)
    same_head = _div(_iota((hq, D_MODEL), 0), tq) == _div(_iota((hq, D_MODEL), 1), HEAD_DIM)
    qexp = jnp.where(same_head, qexp, jnp.zeros_like(qexp))
    expand = (_div(_iota((N_HEADS, hq), 1), tq) == _iota((N_HEADS, hq), 0)).astype(F32)
    cq = cq_ref[...]
    s_p = lax.dot_general(kp_ref[...].astype(BF16), qexp, NT, preferred_element_type=F32)
    s_p = s_p + cq - jnp.dot(cp_ref[...], expand, precision=HI, preferred_element_type=F32)
    s_n = lax.dot_general(kn_ref[...], qexp, NT, preferred_element_type=F32)
    s_n = s_n + cq - jnp.dot(cn_ref[...], expand, precision=HI, preferred_element_type=F32)
    causal = _iota((tq, hq), 0) <= jnp.bitwise_and(_iota((tq, hq), 1), tq - 1)
    s_n = jnp.where(causal, s_n, NEG)
    m = jnp.maximum(jnp.max(s_p, axis=0, keepdims=True), jnp.max(s_n, axis=0, keepdims=True))
    p_p = jnp.exp(s_p - m)
    p_n = jnp.exp(s_n - m)
    inv = 1.0 / (jnp.sum(p_p, axis=0, keepdims=True) + jnp.sum(p_n, axis=0, keepdims=True))
    o_all = (lax.dot_general((p_p * inv).astype(BF16), vp_ref[...].astype(BF16), TN, preferred_element_type=F32)
             + lax.dot_general((p_n * inv).astype(BF16), vn_ref[...], TN, preferred_element_type=F32))
    lane_head = _div(_iota((tq, D_MODEL), 1), HEAD_DIM)
    out = jnp.zeros((tq, D_MODEL), F32)
    for h in range(N_HEADS):
        out = out + jnp.where(lane_head == h, o_all[tq * h:tq * (h + 1), :], 0.0)
    o_ref[...] = out.astype(o_ref.dtype)


def _fox_sample_call(qb, kb, vb, row0, cache_k, cache_v, c_all):
    B, Tp = cache_k.shape[0], cache_k.shape[1]
    tq = c_all.shape[1] - Tp
    blk0 = row0 // tq
    new = pl.BlockSpec((tq, D_MODEL), lambda b: (blk0 + b, 0))
    past = pl.BlockSpec((None, Tp, D_MODEL), lambda b: (b, 0, 0))
    c_past, c_new = c_all[:, :Tp], c_all[:, Tp:]
    cq = jnp.swapaxes(c_new, 1, 2).reshape(B, 1, N_HEADS * tq)
    return pl.pallas_call(
        _fox_sample_body, grid=(B,),
        in_specs=[new, new, new, past, past, pl.BlockSpec((None, Tp, N_HEADS), lambda b: (b, 0, 0)),
                  pl.BlockSpec((None, tq, N_HEADS), lambda b: (b, 0, 0)),
                  pl.BlockSpec((None, 1, N_HEADS * tq), lambda b: (b, 0, 0))],
        out_specs=pl.BlockSpec((tq, D_MODEL), lambda b: (b, 0)),
        out_shape=jax.ShapeDtypeStruct((B * tq, D_MODEL), BF16),
        compiler_params=_params("arbitrary"), name="fox_sample",
    )(qb, kb, vb, cache_k.reshape(B, Tp, D_MODEL), cache_v.reshape(B, Tp, D_MODEL), c_past, c_new, cq)


def _oproj_ln_body(o_ref, x_ref, w_ref, g_ref, b_ref, y_ref):
    mix = jnp.dot(o_ref[...], w_ref[...], preferred_element_type=F32)
    y_ref[...] = _layer_norm(DN_ALPHA * x_ref[...] + mix, g_ref[...], b_ref[...])


def _oproj_ln_call(o, x, w_o, g, b):
    n = x.shape[0]
    rows = pl.BlockSpec((ROW_BLOCK, D_MODEL), lambda i: (i, 0))
    full = lambda a: pl.BlockSpec(a.shape, lambda i: (0,) * a.ndim)
    w = _head_tiles(w_o, 0).astype(BF16)
    g, b = g.reshape(1, -1), b.reshape(1, -1)
    return pl.pallas_call(
        _oproj_ln_body, grid=(n // ROW_BLOCK,),
        in_specs=[pl.BlockSpec((ROW_BLOCK, o.shape[1]), lambda i: (i, 0)), rows, full(w), full(g), full(b)],
        out_specs=rows,
        out_shape=jax.ShapeDtypeStruct((n, D_MODEL), F32),
        compiler_params=_params("arbitrary"), name="oproj_ln",
    )(o, x, w, g, b)


def _group_major(x, T):
    R = x.shape[0] // T
    return x.reshape(R, T, SSM_GROUPS, SSM_GROUP).transpose(2, 0, 1, 3).reshape(SSM_GROUPS, R, T * SSM_GROUP)


def _row_major(yg, T):
    G, R, _ = yg.shape
    return yg.reshape(G, R, T, SSM_GROUP).transpose(1, 2, 0, 3).reshape(R * T, D_MODEL)


def kernel(x_prompt, x_sample, state_ssm_re, state_ssm_im, cache_k, cache_v, cache_logf, meta_tokens, ssm_lam_re, ssm_lam_im, ssm_log_dt, ssm_b_re, ssm_b_im, ssm_c_re, ssm_c_im, ssm_d, w_glu, w_kvf, b_f, w_q, w_o, peer_w_q, peer_k1, peer_k2, peer_u, peer_v, ln1_g, ln1_b, ln2_g, ln2_b):
    bp, seq, _ = x_prompt.shape
    bs, ts, _ = x_sample.shape
    assert bp == 1, "the token stream holds one new stream"
    n_p = N_META + seq
    n_s = bs * ts
    n_rows = -(-(n_p + n_s) // ROW_BLOCK) * ROW_BLOCK
    assert n_p % ts == 0 and n_rows % PEER_ROWS == 0 and n_rows % ATT_BLOCK == 0

    xp = jnp.concatenate([meta_tokens.astype(F32), x_prompt[0]], axis=0)
    xs = x_sample.reshape(n_s, D_MODEL)
    x0 = jnp.concatenate([xp, xs, jnp.zeros((n_rows - n_p - n_s, D_MODEL), F32)], axis=0)

    ssm = (ssm_lam_re[0], ssm_lam_im[0], ssm_log_dt[0], ssm_b_re[0], ssm_b_im[0], ssm_c_re[0], ssm_c_im[0], ssm_d[0])
    n_chunks = -(-n_p // (CHUNK * 8)) * 8
    pad = n_chunks * CHUNK - n_p
    ug_p = _group_major(jnp.pad(xp, ((pad, 0), (0, 0))), CHUNK)
    zero_h = jnp.zeros((SSM_GROUPS, 1, SSM_STATE), F32)
    yg_p, hre_p, him_p = _s5_call(ug_p, zero_h, zero_h, *ssm, T=CHUNK, chain=True)
    y_p = _row_major(yg_p, CHUNK)[pad:]
    ug_s = x_sample.reshape(bs, ts, SSM_GROUPS, SSM_GROUP).transpose(2, 0, 1, 3).reshape(SSM_GROUPS, bs, ts * SSM_GROUP)
    h0re = jnp.swapaxes(state_ssm_re[:, 0], 0, 1)
    h0im = jnp.swapaxes(state_ssm_im[:, 0], 0, 1)
    yg_s, hre_s, him_s = _s5_call(ug_s, h0re, h0im, *ssm, T=ts, chain=False)
    y_s = yg_s.reshape(SSM_GROUPS, bs, ts, SSM_GROUP).transpose(1, 2, 0, 3).reshape(n_s, D_MODEL)
    y0 = jnp.concatenate([y_p, y_s, jnp.zeros((n_rows - n_p - n_s, D_MODEL), F32)], axis=0)

    x1 = _glu_ln_call(y0, x0, w_glu[0], ln1_g[0], ln1_b[0])
    x2 = _peer_ln_call(x1, peer_w_q[0], peer_k1[0], peer_k2[0], peer_u[0], peer_v[0], ln2_g[0], ln2_b[0])

    k_all, v_all, lf_all, kb, vb, qb = _kvq_call(x2, w_kvf, b_f, w_q[0])
    qa, ka, va = _tiles_call(x2, lf_all, w_q[0], w_kvf[:, :D_MODEL], w_kvf[:, D_MODEL:2 * D_MODEL])
    o_p = _fox_prompt_call(qa, ka, va)
    lf_s = lf_all[n_p:n_p + n_s].reshape(bs, ts, N_HEADS)
    lf_cat = jnp.concatenate([cache_logf.astype(F32), lf_s], axis=1)
    c_s = _cumsum_call(lf_cat, lf_cat.shape[1] // 2)
    o_s = _fox_sample_call(qb, kb, vb, n_p, cache_k, cache_v, c_s)
    o_all = lax.dynamic_update_slice(o_p, _head_tiles(o_s, 1), (n_p, 0))
    x3 = _oproj_ln_call(o_all, x2, w_o[0], ln1_g[1], ln1_b[1])
    x4 = _peer_ln_call(x3, peer_w_q[1], peer_k1[1], peer_k2[1], peer_u[1], peer_v[1], ln2_g[1], ln2_b[1])

    heads = lambda a, b, t: a.reshape(b, t, N_HEADS, HEAD_DIM)
    y_prompt = x4[N_META:n_p].reshape(bp, seq, D_MODEL)
    y_sample = x4[n_p:n_p + n_s].reshape(bs, ts, D_MODEL)
    st_p = lambda h: h.reshape(1, 1, SSM_GROUPS, SSM_STATE)
    st_s = lambda h: jnp.swapaxes(h, 0, 1).reshape(bs, 1, SSM_GROUPS, SSM_STATE)
    return (y_prompt, y_sample, st_p(hre_p), st_p(him_p),
            heads(k_all[:n_p], bp, n_p), heads(v_all[:n_p], bp, n_p), lf_all[:n_p].reshape(bp, n_p, N_HEADS),
            st_s(hre_s), st_s(him_s),
            heads(k_all[n_p:n_p + n_s], bs, ts), heads(v_all[n_p:n_p + n_s], bs, ts), lf_s)
```

```python
import functools
import math

import jax
import jax.numpy as jnp
from jax import lax
from jax.experimental import pallas as pl
from jax.experimental.pallas import tpu as pltpu

D_MODEL = 1024
N_META = 16
CHUNK = 64
SSM_GROUP = 16
SSM_GROUPS = D_MODEL // SSM_GROUP
SSM_STATE = 64
N_HEADS = 16
HEAD_DIM = D_MODEL // N_HEADS
ATTN_SCALE = HEAD_DIM ** -0.5
PEER_HEADS = 8
PEER_NKEYS = 128
PEER_TOPK = 16
PEER_DKEY = 256
PEER_HALF = PEER_DKEY // 2
LN_EPS = 1e-5
DEPTH = 2
DN_ALPHA = (2.0 * DEPTH) ** 0.25

VMEM_LIMIT_BYTES = 56 * 1024 * 1024
ROW_BLOCK = 512
PEER_ROWS = 512
PEER_I1_PER_STEP = 8
ATT_BLOCK = 512
TILE_ROWS = 256
HEAD_TILE = 128
BIAS_PIECES = 3
NEG = -1e30

F32 = jnp.float32
BF16 = jnp.bfloat16
HI = lax.Precision.HIGHEST
NT = (((1,), (1,)), ((), ()))
TN = (((0,), (0,)), ((), ()))


def _params(*sem):
    return pltpu.CompilerParams(dimension_semantics=sem, vmem_limit_bytes=VMEM_LIMIT_BYTES)


def _layer_norm(x, g, b):
    mu = jnp.mean(x, axis=-1, keepdims=True)
    xc = x - mu
    var = jnp.mean(xc * xc, axis=-1, keepdims=True)
    return xc * lax.rsqrt(var + LN_EPS) * g + b


def _iota(shape, dim):
    return lax.broadcasted_iota(jnp.int32, shape, dim)


def _div(x, d):
    assert d & (d - 1) == 0
    return jnp.right_shift(x, d.bit_length() - 1)


def _dot_split(a, b):
    ah = a.astype(BF16)
    al = (a - ah.astype(F32)).astype(BF16)
    bh = b.astype(BF16)
    bl = (b - bh.astype(F32)).astype(BF16)
    dot = functools.partial(jnp.dot, preferred_element_type=F32)
    return dot(ah, bh) + (dot(ah, bl) + dot(al, bh))


def _s5_body(T, chain, u_ref, lre_ref, lim_ref, ldt_ref, btre_ref, btim_ref, cre_ref, cim_ref,
             dcol_ref, h0re_ref, h0im_ref, y_ref, hre_ref, him_ref, m_scr, *scan_scr):
    TC = T * SSM_GROUP
    dt = jnp.exp(ldt_ref[...])
    lr = lre_ref[...]
    li = lim_ref[...]
    mag = jnp.exp(lr * dt)
    abr = mag * jnp.cos(li * dt)
    abi = mag * jnp.sin(li * dt)
    nre = abr - 1.0
    den = lr * lr + li * li
    fre = (nre * lr + abi * li) / den
    fim = (abi * lr - nre * li) / den

    def powers(tau):
        pm = jnp.exp(lr * dt * tau)
        ang = li * dt * tau
        return pm * jnp.cos(ang), pm * jnp.sin(ang)

    tau = _iota((T, SSM_STATE), 0).astype(F32)
    pwr, pwi = powers(tau)
    rvr, rvi = powers((T - 1.0) - tau)
    pfr, pfi = pwr * fre - pwi * fim, pwr * fim + pwi * fre
    par, pai = pwr * abr - pwi * abi, pwr * abi + pwi * abr
    rfr, rfi = rvr * fre - rvi * fim, rvr * fim + rvi * fre

    rep_m = (jnp.right_shift(_iota((TC, T), 0), 4) == _iota((TC, T), 1)).astype(F32)
    til_m = (jnp.bitwise_and(_iota((TC, SSM_GROUP), 0), SSM_GROUP - 1) == _iota((TC, SSM_GROUP), 1)).astype(F32)
    rep = lambda a: jnp.dot(rep_m, a, precision=HI, preferred_element_type=F32)
    til = lambda a: jnp.dot(til_m, a, precision=HI, preferred_element_type=F32)
    c_re, c_im = til(cre_ref[...]), til(cim_ref[...])
    b_re, b_im = til(btre_ref[...]), til(btim_ref[...])

    pfr_r, pfi_r = rep(pfr), rep(pfi)
    caf_re = c_re * pfr_r - c_im * pfi_r
    caf_im = c_re * pfi_r + c_im * pfr_r
    kt = (lax.dot_general(btre_ref[...], caf_re, NT, precision=HI, preferred_element_type=F32)
          - lax.dot_general(btim_ref[...], caf_im, NT, precision=HI, preferred_element_type=F32))
    row = _iota((SSM_GROUP, TC), 0)
    lane = _iota((SSM_GROUP, TC), 1)
    kt = kt + jnp.where(lane == row, dcol_ref[...], 0.0)
    for s in range(T):
        shifted = pltpu.roll(kt, SSM_GROUP * s, axis=1) if s else kt
        m_scr[SSM_GROUP * s:SSM_GROUP * (s + 1), :] = jnp.where(lane >= SSM_GROUP * s, shifted, 0.0)

    u = u_ref[...]
    rfr_r, rfi_r = rep(rfr), rep(rfi)
    wm_re = rfr_r * b_re - rfi_r * b_im
    wm_im = rfr_r * b_im + rfi_r * b_re
    s_re = jnp.dot(u, wm_re, precision=HI, preferred_element_type=F32)
    s_im = jnp.dot(u, wm_im, precision=HI, preferred_element_type=F32)
    atr, ati = powers(float(T))

    if chain:
        sre_scr, sim_scr, hpre_scr, hpim_scr = scan_scr
        sre_scr[...] = s_re
        sim_scr[...] = s_im

        def step(c, carry):
            hr, hi = carry
            hpre_scr[pl.ds(c, 1), :] = hr
            hpim_scr[pl.ds(c, 1), :] = hi
            sr = sre_scr[pl.ds(c, 1), :]
            si = sim_scr[pl.ds(c, 1), :]
            return atr * hr - ati * hi + sr, atr * hi + ati * hr + si

        hr, hi = lax.fori_loop(0, u.shape[0], step, (h0re_ref[...], h0im_ref[...]))
        hre_ref[...] = hr
        him_ref[...] = hi
        hp_re, hp_im = hpre_scr[...], hpim_scr[...]
    else:
        hp_re, hp_im = h0re_ref[...], h0im_ref[...]
        hre_ref[...] = atr * hp_re - ati * hp_im + s_re
        him_ref[...] = atr * hp_im + ati * hp_re + s_im

    par_r, pai_r = rep(par), rep(pai)
    ca1_re = c_re * par_r - c_im * pai_r
    ca1_im = c_re * pai_r + c_im * par_r
    y = (_dot_split(u, m_scr[...])
         + lax.dot_general(hp_re, ca1_re, NT, precision=HI, preferred_element_type=F32)
         - lax.dot_general(hp_im, ca1_im, NT, precision=HI, preferred_element_type=F32))
    y_ref[...] = jax.nn.gelu(y)


def _s5_call(ug, h0re, h0im, lam_re, lam_im, log_dt, b_re, b_im, c_re, c_im, d_skip, T, chain):
    G, R, TC = ug.shape
    R0 = h0re.shape[1]
    P = SSM_STATE
    vec = lambda a: a.reshape(G, 1, -1)
    bt = lambda a: jnp.swapaxes(a, 1, 2)
    grp = lambda *shape: pl.BlockSpec((None,) + shape, lambda g: (g,) + (0,) * len(shape))
    scratch = [pltpu.VMEM((TC, TC), F32)]
    if chain:
        scratch += [pltpu.VMEM((R, P), F32)] * 4
    return pl.pallas_call(
        functools.partial(_s5_body, T, chain),
        grid=(G,),
        in_specs=[grp(R, TC), grp(1, P), grp(1, P), grp(1, 1), grp(SSM_GROUP, P), grp(SSM_GROUP, P),
                  grp(SSM_GROUP, P), grp(SSM_GROUP, P), grp(SSM_GROUP, 1), grp(R0, P), grp(R0, P)],
        out_specs=[grp(R, TC), grp(R0, P), grp(R0, P)],
        out_shape=[jax.ShapeDtypeStruct((G, R, TC), F32), jax.ShapeDtypeStruct((G, R0, P), F32),
                   jax.ShapeDtypeStruct((G, R0, P), F32)],
        scratch_shapes=scratch,
        compiler_params=_params("arbitrary"),
        name="s5_chain" if chain else "s5_step",
    )(ug, vec(lam_re), vec(lam_im), log_dt.reshape(G, 1, 1), bt(b_re), bt(b_im), c_re, c_im,
      d_skip.reshape(G, SSM_GROUP, 1), h0re, h0im)


def _glu_ln_body(y_ref, x_ref, w_ref, g_ref, b_ref, o_ref):
    z = jnp.dot(y_ref[...].astype(BF16), w_ref[...], preferred_element_type=F32)
    mix = z[:, :D_MODEL] * jax.nn.sigmoid(z[:, D_MODEL:])
    o_ref[...] = _layer_norm(DN_ALPHA * x_ref[...] + mix, g_ref[...], b_ref[...])


def _glu_ln_call(y, x, w_glu, g, b):
    n = x.shape[0]
    rows = pl.BlockSpec((ROW_BLOCK, D_MODEL), lambda i: (i, 0))
    full = lambda a: pl.BlockSpec(a.shape, lambda i: (0,) * a.ndim)
    w = w_glu.astype(BF16)
    g, b = g.reshape(1, -1), b.reshape(1, -1)
    return pl.pallas_call(
        _glu_ln_body, grid=(n // ROW_BLOCK,),
        in_specs=[rows, rows, full(w), full(g), full(b)], out_specs=rows,
        out_shape=jax.ShapeDtypeStruct((n, D_MODEL), F32),
        compiler_params=_params("arbitrary"), name="glu_ln",
    )(y, x, w, g, b)


PEER_RANKS = PEER_TOPK + 1
PEER_RANK_ROWS = 24


def _top_rows(s):
    ridx = _iota((PEER_RANK_ROWS, s.shape[1]), 0)

    def body(r, carry):
        work, out = carry
        m = jnp.max(work, axis=0, keepdims=True)
        out = jnp.where(ridx == r, m, out)
        return jnp.where(work == m, NEG, work), out

    _, out = lax.fori_loop(0, PEER_RANKS, body, (s, jnp.full((PEER_RANK_ROWS, s.shape[1]), NEG, F32)))
    return out


def _pair_candidates(a, b):
    n = a.shape[1]
    rows8 = _iota((8, n), 0)
    parts = [a[0:1, :] + b]
    for i in range(1, 8):
        part = a[i:i + 1, :] + b[0:8, :]
        keep = PEER_RANKS // (i + 1)
        parts.append(part if keep >= 8 else jnp.where(rows8 < keep, part, NEG))
    parts.append(a[8:PEER_RANK_ROWS, :] + b[0:1, :])
    return jnp.concatenate(parts, axis=0)


def _peer_body(x_ref, wq_ref, k1_ref, k2_ref, u_ref, v_ref, g_ref, b_ref, o_ref,
               xb_scr, q_scr, s1_scr, s2_scr, e1_scr, e2_scr, tau_scr, acc_scr):
    e = pl.program_id(1)
    n_e = pl.num_programs(1)

    @pl.when(e == 0)
    def _select():
        xb = x_ref[...].astype(BF16)
        xb_scr[...] = xb
        q = jnp.dot(xb, wq_ref[...], preferred_element_type=F32)
        for j in range(2 * PEER_HEADS):
            q_scr[j] = q[:, PEER_HALF * j:PEER_HALF * (j + 1)]

        def head(h, carry):
            s1 = lax.dot_general(k1_ref[...], q_scr[2 * h], NT, precision=HI, preferred_element_type=F32)
            s2 = lax.dot_general(k2_ref[...], q_scr[2 * h + 1], NT, precision=HI, preferred_element_type=F32)
            a = _top_rows(s1)
            b = _top_rows(s2)
            best = _top_rows(_pair_candidates(a, b))
            top = best[0:1, :]
            chosen = _iota(best.shape, 0) < PEER_TOPK
            z = jnp.sum(jnp.where(chosen, jnp.exp(best - top), 0.0), axis=0, keepdims=True)
            s1_scr[h] = s1
            s2_scr[h] = s2
            e1_scr[h] = jnp.exp(s1 - a[0:1, :]) / z
            e2_scr[h] = jnp.exp(s2 - b[0:1, :])
            tau_scr[h] = 0.5 * (best[PEER_TOPK - 1:PEER_TOPK, :] + best[PEER_TOPK:PEER_TOPK + 1, :])
            return carry

        lax.fori_loop(0, PEER_HEADS, head, 0)
        acc_scr[...] = jnp.zeros_like(acc_scr)

    ux = lax.dot_general(u_ref[...], xb_scr[...], NT, preferred_element_type=F32)
    w_rows = []
    for il in range(PEER_I1_PER_STEP):
        i1 = e * PEER_I1_PER_STEP + il
        gate = None
        for h in range(PEER_HEADS):
            theta = tau_scr[h] - s1_scr[h, pl.ds(i1, 1), :]
            term = jnp.where(s2_scr[h] >= theta, e2_scr[h], 0.0) * e1_scr[h, pl.ds(i1, 1), :]
            gate = term if gate is None else gate + term
        act = jax.nn.gelu(ux[il * PEER_NKEYS:(il + 1) * PEER_NKEYS, :])
        w_rows.append((gate * act).astype(BF16))
    w = jnp.concatenate(w_rows, axis=0)
    acc_scr[...] += lax.dot_general(w, v_ref[...], TN, preferred_element_type=F32)

    @pl.when(e == n_e - 1)
    def _finish():
        o_ref[...] = _layer_norm(DN_ALPHA * x_ref[...] + acc_scr[...], g_ref[...], b_ref[...])


def _peer_ln_call(x, w_pq, k1, k2, u_tab, v_tab, g, b):
    n = x.shape[0]
    nr = PEER_ROWS
    eb = PEER_I1_PER_STEP * PEER_NKEYS
    rows = pl.BlockSpec((nr, D_MODEL), lambda i, e: (i, 0))
    full = lambda a: pl.BlockSpec(a.shape, lambda i, e: (0,) * a.ndim)
    tab = pl.BlockSpec((eb, D_MODEL), lambda i, e: (e, 0))
    wq = w_pq.astype(BF16)
    g, b = g.reshape(1, -1), b.reshape(1, -1)
    per_head = pltpu.VMEM((PEER_HEADS, PEER_NKEYS, nr), F32)
    return pl.pallas_call(
        _peer_body, grid=(n // nr, PEER_NKEYS // PEER_I1_PER_STEP),
        in_specs=[rows, full(wq), full(k1), full(k2), tab, tab, full(g), full(b)], out_specs=rows,
        out_shape=jax.ShapeDtypeStruct((n, D_MODEL), F32),
        scratch_shapes=[pltpu.VMEM((nr, D_MODEL), BF16), pltpu.VMEM((2 * PEER_HEADS, nr, PEER_HALF), F32),
                        per_head, per_head, per_head, per_head, pltpu.VMEM((PEER_HEADS, 1, nr), F32),
                        pltpu.VMEM((nr, D_MODEL), F32)],
        compiler_params=_params("arbitrary", "arbitrary"), name="peer_ln",
    )(x, wq, k1, k2, u_tab.astype(BF16), v_tab.astype(BF16), g, b)


def _kvq_body(x_ref, wkvq_ref, wf_ref, bf_ref, k_ref, v_ref, lf_ref, kb_ref, vb_ref, qb_ref):
    x = x_ref[...]
    z = jnp.dot(x.astype(BF16), wkvq_ref[...], preferred_element_type=F32)
    k = z[:, :D_MODEL]
    v = z[:, D_MODEL:2 * D_MODEL]
    k_ref[...] = k
    v_ref[...] = v
    kb_ref[...] = k.astype(BF16)
    vb_ref[...] = v.astype(BF16)
    qb_ref[...] = (z[:, 2 * D_MODEL:] * ATTN_SCALE).astype(BF16)
    zf = jnp.dot(x, wf_ref[...], precision=HI, preferred_element_type=F32)[:, :N_HEADS] + bf_ref[...]
    lf_ref[...] = jnp.minimum(zf, 0.0) - jnp.log(1.0 + jnp.exp(-jnp.abs(zf)))


def _kvq_call(x, w_kvf, b_f, w_q):
    n = x.shape[0]
    rows = pl.BlockSpec((ROW_BLOCK, D_MODEL), lambda i: (i, 0))
    rows_h = pl.BlockSpec((ROW_BLOCK, N_HEADS), lambda i: (i, 0))
    full = lambda a: pl.BlockSpec(a.shape, lambda i: (0,) * a.ndim)
    wkvq = jnp.concatenate([w_kvf[:, :2 * D_MODEL], w_q], axis=1).astype(BF16)
    wf = jnp.pad(w_kvf[:, 2 * D_MODEL:], ((0, 0), (0, 128 - N_HEADS)))
    bf = b_f.reshape(1, N_HEADS)
    f32 = jax.ShapeDtypeStruct((n, D_MODEL), F32)
    b16 = jax.ShapeDtypeStruct((n, D_MODEL), BF16)
    return pl.pallas_call(
        _kvq_body, grid=(n // ROW_BLOCK,),
        in_specs=[rows, full(wkvq), full(wf), full(bf)],
        out_specs=[rows, rows, rows_h, rows, rows, rows],
        out_shape=[f32, f32, jax.ShapeDtypeStruct((n, N_HEADS), F32), b16, b16, b16],
        compiler_params=_params("arbitrary"), name="kvq",
    )(x, wkvq, wf, bf)


def _cumsum_body(lf_ref, c_ref, carry_scr):
    @pl.when(pl.program_id(1) == 0)
    def _():
        carry_scr[...] = jnp.zeros_like(carry_scr)

    n = lf_ref.shape[0]
    tri = (_iota((n, n), 1) <= _iota((n, n), 0)).astype(F32)
    c = jnp.dot(tri, lf_ref[...], precision=HI, preferred_element_type=F32) + carry_scr[...]
    c_ref[...] = c
    carry_scr[...] = c[n - 1:n, :]


def _cumsum_call(lf, block):
    B, T, Hh = lf.shape
    spec = pl.BlockSpec((None, block, Hh), lambda b, i: (b, i, 0))
    return pl.pallas_call(
        _cumsum_body, grid=(B, T // block), in_specs=[spec], out_specs=spec,
        out_shape=jax.ShapeDtypeStruct(lf.shape, F32),
        scratch_shapes=[pltpu.VMEM((1, Hh), F32)],
        compiler_params=_params("arbitrary", "arbitrary"), name="cumsum_logf",
    )(lf)


def _head_tiles(w, axis):
    shape = w.shape[:axis] + (N_HEADS, HEAD_DIM) + w.shape[axis + 1:]
    pad = [(0, 0)] * (len(shape))
    pad[axis + 1] = (0, HEAD_TILE - HEAD_DIM)
    out = jnp.pad(w.reshape(shape), pad)
    return out.reshape(w.shape[:axis] + (N_HEADS * HEAD_TILE,) + w.shape[axis + 1:])


def _tiles_body(x_ref, lf_ref, w_ref, sq_ref, sk_ref, oq_ref, ok_ref, ov_ref, qa_ref, ka_ref, va_ref, carry_scr):
    @pl.when(pl.program_id(0) == 0)
    def _():
        carry_scr[...] = jnp.zeros_like(carry_scr)

    n = x_ref.shape[0]
    wt = N_HEADS * HEAD_TILE
    tri = (_iota((n, n), 1) <= _iota((n, n), 0)).astype(F32)
    c = jnp.dot(tri, lf_ref[...], precision=HI, preferred_element_type=F32) + carry_scr[...]
    carry_scr[...] = c[n - 1:n, :]
    hi = c.astype(BF16)
    rest = c - hi.astype(F32)
    mid = rest.astype(BF16)
    lo = (rest - mid.astype(F32)).astype(BF16)
    pieces = (hi, mid, lo)
    q_bias = sum(jnp.dot(p, sq_ref[i], preferred_element_type=F32) for i, p in enumerate(pieces))
    k_bias = sum(jnp.dot(p, sk_ref[i], preferred_element_type=F32) for i, p in enumerate(pieces))
    z = jnp.dot(x_ref[...].astype(BF16), w_ref[...], preferred_element_type=F32)
    qa_ref[...] = (z[:, :wt] * ATTN_SCALE + q_bias + oq_ref[...]).astype(BF16)
    ka_ref[...] = (z[:, wt:2 * wt] - k_bias + ok_ref[...]).astype(BF16)
    va_ref[...] = (z[:, 2 * wt:] + ov_ref[...]).astype(BF16)


def _tiles_call(x, lf, w_q, w_k, w_v):
    n = x.shape[0]
    wt = N_HEADS * HEAD_TILE
    nr = TILE_ROWS
    w = jnp.concatenate([_head_tiles(w_q, 1), _head_tiles(w_k, 1), _head_tiles(w_v, 1)], axis=1).astype(BF16)
    lane = jnp.arange(wt)
    head, off = lane // HEAD_TILE, lane % HEAD_TILE
    at = lambda o: ((head[None, None, :] == jnp.arange(N_HEADS)[None, :, None])
                    & (off[None, None, :] == o + jnp.arange(BIAS_PIECES)[:, None, None])).astype(BF16)
    sq, sk = at(HEAD_DIM), at(HEAD_DIM + BIAS_PIECES)
    ones = lambda lo, hi: ((off >= lo) & (off < hi)).astype(F32).reshape(1, wt)
    oq = ones(HEAD_DIM + BIAS_PIECES, HEAD_DIM + 2 * BIAS_PIECES)
    ok = ones(HEAD_DIM, HEAD_DIM + BIAS_PIECES)
    ov = ones(HEAD_DIM, HEAD_DIM + 1)
    full = lambda a: pl.BlockSpec(a.shape, lambda i: (0,) * a.ndim)
    tiles = pl.BlockSpec((nr, wt), lambda i: (i, 0))
    out = jax.ShapeDtypeStruct((n, wt), BF16)
    return pl.pallas_call(
        _tiles_body, grid=(n // nr,),
        in_specs=[pl.BlockSpec((nr, D_MODEL), lambda i: (i, 0)), pl.BlockSpec((nr, N_HEADS), lambda i: (i, 0)),
                  full(w), full(sq), full(sk), full(oq), full(ok), full(ov)],
        out_specs=[tiles, tiles, tiles], out_shape=[out, out, out],
        scratch_shapes=[pltpu.VMEM((1, N_HEADS), F32)],
        compiler_params=_params("arbitrary"), name="qkv_tiles",
    )(x, lf, w, sq, sk, oq, ok, ov)


def _fox_prompt_body(q_ref, k_ref, v_ref, o_ref, m_scr, acc_scr):
    i = pl.program_id(0)
    j = pl.program_id(1)
    tq, tk = q_ref.shape[0], k_ref.shape[0]

    @pl.when(j == 0)
    def _init():
        m_scr[...] = jnp.full_like(m_scr, NEG)
        acc_scr[...] = jnp.zeros_like(acc_scr)

    def block(diagonal):
        for h in range(N_HEADS):
            sl = slice(HEAD_TILE * h, HEAD_TILE * (h + 1))
            s = lax.dot_general(q_ref[:, sl], k_ref[:, sl], NT, preferred_element_type=F32)
            if diagonal:
                s = jnp.where(_iota((tq, tk), 1) <= _iota((tq, tk), 0), s, NEG)
            m_prev = m_scr[h]
            m_new = jnp.maximum(m_prev, jnp.max(s, axis=1, keepdims=True))
            p = jnp.exp(s - jnp.tile(m_new, (1, tk // HEAD_TILE))).astype(BF16)
            acc_scr[h] = (jnp.exp(m_prev - m_new) * acc_scr[h]
                          + jnp.dot(p, v_ref[:, sl], preferred_element_type=F32))
            m_scr[h] = m_new

    pl.when(j < i)(lambda: block(False))

    @pl.when(j == i)
    def _last():
        block(True)
        for h in range(N_HEADS):
            acc = acc_scr[h]
            o_ref[:, HEAD_TILE * h:HEAD_TILE * (h + 1)] = (acc / acc[:, HEAD_DIM:HEAD_DIM + 1]).astype(o_ref.dtype)


def _fox_prompt_call(qa, ka, va):
    n, wt = qa.shape
    t = ATT_BLOCK
    nb = n // t
    qrow = pl.BlockSpec((t, wt), lambda i, j: (i, 0))
    krow = pl.BlockSpec((t, wt), lambda i, j: (jnp.minimum(i, j), 0))
    return pl.pallas_call(
        _fox_prompt_body, grid=(nb, nb),
        in_specs=[qrow, krow, krow], out_specs=qrow,
        out_shape=jax.ShapeDtypeStruct((n, wt), BF16),
        scratch_shapes=[pltpu.VMEM((N_HEADS, t, HEAD_TILE), F32), pltpu.VMEM((N_HEADS, t, HEAD_TILE), F32)],
        compiler_params=_params("arbitrary", "arbitrary"), name="fox_prompt",
    )(qa, ka, va)


def _fox_sample_body(q_ref, kn_ref, vn_ref, kp_ref, vp_ref, cp_ref, cn_ref, cq_ref, o_ref):
    tq = q_ref.shape[0]
    hq = N_HEADS * tq
    q = q_ref[...]
    qexp = jnp.concatenate([q] * N_HEADS, axis=0)
    same_head = _div(_iota((hq, D_MODEL), 0), tq) == _div(_iota((hq, D_MODEL), 1), HEAD_DIM)
    qexp = jnp.where(same_head, qexp, jnp.zeros_like(qexp))
    expand = (_div(_iota((N_HEADS, hq), 1), tq) == _iota((N_HEADS, hq), 0)).astype(F32)
    cq = cq_ref[...]
    s_p = lax.dot_general(kp_ref[...].astype(BF16), qexp, NT, preferred_element_type=F32)
    s_p = s_p + cq - jnp.dot(cp_ref[...], expand, precision=HI, preferred_element_type=F32)
    s_n = lax.dot_general(kn_ref[...], qexp, NT, preferred_element_type=F32)
    s_n = s_n + cq - jnp.dot(cn_ref[...], expand, precision=HI, preferred_element_type=F32)
    causal = _iota((tq, hq), 0) <= jnp.bitwise_and(_iota((tq, hq), 1), tq - 1)
    s_n = jnp.where(causal, s_n, NEG)
    m = jnp.maximum(jnp.max(s_p, axis=0, keepdims=True), jnp.max(s_n, axis=0, keepdims=True))
    p_p = jnp.exp(s_p - m)
    p_n = jnp.exp(s_n - m)
    inv = 1.0 / (jnp.sum(p_p, axis=0, keepdims=True) + jnp.sum(p_n, axis=0, keepdims=True))
    o_all = (lax.dot_general((p_p * inv).astype(BF16), vp_ref[...].astype(BF16), TN, preferred_element_type=F32)
             + lax.dot_general((p_n * inv).astype(BF16), vn_ref[...], TN, preferred_element_type=F32))
    lane_head = _div(_iota((tq, D_MODEL), 1), HEAD_DIM)
    out = jnp.zeros((tq, D_MODEL), F32)
    for h in range(N_HEADS):
        out = out + jnp.where(lane_head == h, o_all[tq * h:tq * (h + 1), :], 0.0)
    o_ref[...] = out.astype(o_ref.dtype)


def _fox_sample_call(qb, kb, vb, row0, cache_k, cache_v, c_all):
    B, Tp = cache_k.shape[0], cache_k.shape[1]
    tq = c_all.shape[1] - Tp
    blk0 = row0 // tq
    new = pl.BlockSpec((tq, D_MODEL), lambda b: (blk0 + b, 0))
    past = pl.BlockSpec((None, Tp, D_MODEL), lambda b: (b, 0, 0))
    c_past, c_new = c_all[:, :Tp], c_all[:, Tp:]
    cq = jnp.swapaxes(c_new, 1, 2).reshape(B, 1, N_HEADS * tq)
    return pl.pallas_call(
        _fox_sample_body, grid=(B,),
        in_specs=[new, new, new, past, past, pl.BlockSpec((None, Tp, N_HEADS), lambda b: (b, 0, 0)),
                  pl.BlockSpec((None, tq, N_HEADS), lambda b: (b, 0, 0)),
                  pl.BlockSpec((None, 1, N_HEADS * tq), lambda b: (b, 0, 0))],
        out_specs=pl.BlockSpec((tq, D_MODEL), lambda b: (b, 0)),
        out_shape=jax.ShapeDtypeStruct((B * tq, D_MODEL), BF16),
        compiler_params=_params("arbitrary"), name="fox_sample",
    )(qb, kb, vb, cache_k.reshape(B, Tp, D_MODEL), cache_v.reshape(B, Tp, D_MODEL), c_past, c_new, cq)


def _oproj_ln_body(o_ref, x_ref, w_ref, g_ref, b_ref, y_ref):
    mix = jnp.dot(o_ref[...], w_ref[...], preferred_element_type=F32)
    y_ref[...] = _layer_norm(DN_ALPHA * x_ref[...] + mix, g_ref[...], b_ref[...])


def _oproj_ln_call(o, x, w_o, g, b):
    n = x.shape[0]
    rows = pl.BlockSpec((ROW_BLOCK, D_MODEL), lambda i: (i, 0))
    full = lambda a: pl.BlockSpec(a.shape, lambda i: (0,) * a.ndim)
    w = _head_tiles(w_o, 0).astype(BF16)
    g, b = g.reshape(1, -1), b.reshape(1, -1)
    return pl.pallas_call(
        _oproj_ln_body, grid=(n // ROW_BLOCK,),
        in_specs=[pl.BlockSpec((ROW_BLOCK, o.shape[1]), lambda i: (i, 0)), rows, full(w), full(g), full(b)],
        out_specs=rows,
        out_shape=jax.ShapeDtypeStruct((n, D_MODEL), F32),
        compiler_params=_params("arbitrary"), name="oproj_ln",
    )(o, x, w, g, b)


def _group_major(x, T):
    R = x.shape[0] // T
    return x.reshape(R, T, SSM_GROUPS, SSM_GROUP).transpose(2, 0, 1, 3).reshape(SSM_GROUPS, R, T * SSM_GROUP)


def _row_major(yg, T):
    G, R, _ = yg.shape
    return yg.reshape(G, R, T, SSM_GROUP).transpose(1, 2, 0, 3).reshape(R * T, D_MODEL)


def kernel(x_prompt, x_sample, state_ssm_re, state_ssm_im, cache_k, cache_v, cache_logf, meta_tokens, ssm_lam_re, ssm_lam_im, ssm_log_dt, ssm_b_re, ssm_b_im, ssm_c_re, ssm_c_im, ssm_d, w_glu, w_kvf, b_f, w_q, w_o, peer_w_q, peer_k1, peer_k2, peer_u, peer_v, ln1_g, ln1_b, ln2_g, ln2_b):
    bp, seq, _ = x_prompt.shape
    bs, ts, _ = x_sample.shape
    assert bp == 1, "the token stream holds one new stream"
    n_p = N_META + seq
    n_s = bs * ts
    n_rows = -(-(n_p + n_s) // ROW_BLOCK) * ROW_BLOCK
    assert n_p % ts == 0 and n_rows % PEER_ROWS == 0 and n_rows % ATT_BLOCK == 0

    xp = jnp.concatenate([meta_tokens.astype(F32), x_prompt[0]], axis=0)
    xs = x_sample.reshape(n_s, D_MODEL)
    x0 = jnp.concatenate([xp, xs, jnp.zeros((n_rows - n_p - n_s, D_MODEL), F32)], axis=0)

    ssm = (ssm_lam_re[0], ssm_lam_im[0], ssm_log_dt[0], ssm_b_re[0], ssm_b_im[0], ssm_c_re[0], ssm_c_im[0], ssm_d[0])
    n_chunks = -(-n_p // (CHUNK * 8)) * 8
    pad = n_chunks * CHUNK - n_p
    ug_p = _group_major(jnp.pad(xp, ((pad, 0), (0, 0))), CHUNK)
    zero_h = jnp.zeros((SSM_GROUPS, 1, SSM_STATE), F32)
    yg_p, hre_p, him_p = _s5_call(ug_p, zero_h, zero_h, *ssm, T=CHUNK, chain=True)
    y_p = _row_major(yg_p, CHUNK)[pad:]
    ug_s = x_sample.reshape(bs, ts, SSM_GROUPS, SSM_GROUP).transpose(2, 0, 1, 3).reshape(SSM_GROUPS, bs, ts * SSM_GROUP)
    h0re = jnp.swapaxes(state_ssm_re[:, 0], 0, 1)
    h0im = jnp.swapaxes(state_ssm_im[:, 0], 0, 1)
    yg_s, hre_s, him_s = _s5_call(ug_s, h0re, h0im, *ssm, T=ts, chain=False)
    y_s = yg_s.reshape(SSM_GROUPS, bs, ts, SSM_GROUP).transpose(1, 2, 0, 3).reshape(n_s, D_MODEL)
    y0 = jnp.concatenate([y_p, y_s, jnp.zeros((n_rows - n_p - n_s, D_MODEL), F32)], axis=0)

    x1 = _glu_ln_call(y0, x0, w_glu[0], ln1_g[0], ln1_b[0])
    x2 = _peer_ln_call(x1, peer_w_q[0], peer_k1[0], peer_k2[0], peer_u[0], peer_v[0], ln2_g[0], ln2_b[0])

    k_all, v_all, lf_all, kb, vb, qb = _kvq_call(x2, w_kvf, b_f, w_q[0])
    qa, ka, va = _tiles_call(x2, lf_all, w_q[0], w_kvf[:, :D_MODEL], w_kvf[:, D_MODEL:2 * D_MODEL])
    o_p = _fox_prompt_call(qa, ka, va)
    lf_s = lf_all[n_p:n_p + n_s].reshape(bs, ts, N_HEADS)
    lf_cat = jnp.concatenate([cache_logf.astype(F32), lf_s], axis=1)
    c_s = _cumsum_call(lf_cat, lf_cat.shape[1] // 2)
    o_s = _fox_sample_call(qb, kb, vb, n_p, cache_k, cache_v, c_s)
    o_all = lax.dynamic_update_slice(o_p, _head_tiles(o_s, 1), (n_p, 0))
    x3 = _oproj_ln_call(o_all, x2, w_o[0], ln1_g[1], ln1_b[1])
    x4 = _peer_ln_call(x3, peer_w_q[1], peer_k1[1], peer_k2[1], peer_u[1], peer_v[1], ln2_g[1], ln2_b[1])

    heads = lambda a, b, t: a.reshape(b, t, N_HEADS, HEAD_DIM)
    y_prompt = x4[N_META:n_p].reshape(bp, seq, D_MODEL)
    y_sample = x4[n_p:n_p + n_s].reshape(bs, ts, D_MODEL)
    st_p = lambda h: h.reshape(1, 1, SSM_GROUPS, SSM_STATE)
    st_s = lambda h: jnp.swapaxes(h, 0, 1).reshape(bs, 1, SSM_GROUPS, SSM_STATE)
    return (y_prompt, y_sample, st_p(hre_p), st_p(him_p),
            heads(k_all[:n_p], bp, n_p), heads(v_all[:n_p], bp, n_p), lf_all[:n_p].reshape(bp, n_p, N_HEADS),
            st_s(hre_s), st_s(him_s),
            heads(k_all[n_p:n_p + n_s], bs, ts), heads(v_all[n_p:n_p + n_s], bs, ts), lf_s)
```
